```python
import math
import jax, jax.numpy as jnp
from jax import lax
import numpy as np

D_MODEL = 1024
BATCH = 2
SEQ = 8192
DEPTH = 2

CHUNK = 64
Q_BLOCK = 128
EPS = 1e-6

MLSTM_HEADS = 4
MLSTM_HD = D_MODEL // MLSTM_HEADS
MLSTM_W = MLSTM_HEADS * MLSTM_HD
CONV_K = 4

DIFF_HEADS = 8
DIFF_HD = D_MODEL // (2 * DIFF_HEADS)
DIFF_W = DIFF_HEADS * 2 * DIFF_HD

RET_HEADS = 4
RET_HD = D_MODEL // RET_HEADS
RET_W = RET_HEADS * RET_HD

FFN_HIDDEN = -(-8 * D_MODEL // (3 * 256)) * 256

IN_SIZES = (MLSTM_W, MLSTM_W, MLSTM_W, MLSTM_W, MLSTM_HEADS, MLSTM_HEADS,
            DIFF_W, DIFF_W, DIFF_W,
            RET_W, RET_W, RET_W, RET_W,
            3 * D_MODEL)
N_IN = sum(IN_SIZES)

kernel_name = "hybrid_mlstm_diffattn_retention_block"


def rmsnorm(x, g):
    x32 = x.astype(jnp.float32)
    y = x32 * lax.rsqrt(jnp.mean(x32 * x32, axis=-1, keepdims=True) + EPS)
    return (y * g.astype(jnp.float32)).astype(x.dtype)


def head_rms(x):
    x32 = x.astype(jnp.float32)
    return x32 * lax.rsqrt(jnp.mean(x32 * x32, axis=-1, keepdims=True) + EPS)


def causal_dwconv(x, w):
    K = w.shape[0]
    S = x.shape[1]
    xp = jnp.pad(x, ((0, 0), (K - 1, 0), (0, 0)))
    out = xp[:, 0:S] * w[0]
    for j in range(1, K):
        out = out + xp[:, j:j + S] * w[j]
    return out


def to_chunks(a):
    B, S = a.shape[0], a.shape[1]
    return jnp.moveaxis(a.reshape(B, S // CHUNK, CHUNK, *a.shape[2:]), 1, 0)


def from_chunks(a):
    a = jnp.moveaxis(a, 0, 1)
    return a.reshape(a.shape[0], a.shape[1] * a.shape[2], *a.shape[3:])


def mlstm(q, k, v, i_pre, f_pre):
    B, S, H, dh = q.shape
    f32 = jnp.float32
    q = q.astype(f32)
    k = k.astype(f32) * (dh ** -0.5)
    v = v.astype(f32)
    logf = jax.nn.log_sigmoid(f_pre.astype(f32))
    ig = i_pre.astype(f32)
    causal = jnp.tril(jnp.ones((CHUNK, CHUNK), dtype=bool))

    def step(carry, inp):
        C, n, m = carry
        qj, kj, vj, lf, igj = inp
        b = jnp.cumsum(lf, axis=1).transpose(0, 2, 1)
        igT = igj.transpose(0, 2, 1)
        g = b[..., -1]
        d_intra = b[..., :, None] - b[..., None, :] + igT[..., None, :]
        d_intra = jnp.where(causal, d_intra, -jnp.inf)
        d_inter = b + m[..., None]
        m_row = jnp.maximum(jnp.max(d_intra, axis=-1), d_inter)
        w_intra = jnp.exp(d_intra - m_row[..., None])
        w_inter = jnp.exp(d_inter - m_row)
        s = jnp.einsum('blhd,bshd->bhls', qj, kj) * w_intra
        w_int_t = w_inter.transpose(0, 2, 1)[..., None]
        num = (jnp.einsum('bhls,bshe->blhe', s, vj)
               + jnp.einsum('blhd,bhde->blhe', qj, C) * w_int_t)
        den = jnp.sum(s, axis=-1) + jnp.einsum('blhd,bhd->bhl', qj, n) * w_inter
        den = jnp.maximum(jnp.abs(den), jnp.exp(-m_row))
        h = num / den.transpose(0, 2, 1)[..., None]
        a = g[..., None] - b + igT
        m_new = jnp.maximum(g + m, jnp.max(a, axis=-1))
        w_old = jnp.exp(g + m - m_new)
        w_k = jnp.exp(a - m_new[..., None])
        C_new = w_old[..., None, None] * C + jnp.einsum('bhs,bshd,bshe->bhde', w_k, kj, vj)
        n_new = w_old[..., None] * n + jnp.einsum('bhs,bshd->bhd', w_k, kj)
        return (C_new, n_new, m_new), h

    init = (jnp.zeros((B, H, dh, dh), f32), jnp.zeros((B, H, dh), f32), jnp.zeros((B, H), f32))
    xs = (to_chunks(q), to_chunks(k), to_chunks(v), to_chunks(logf), to_chunks(ig))
    _, hs = lax.scan(step, init, xs)
    return from_chunks(hs)


def retention(q, k, v):
    B, S, H, dh = q.shape
    f32 = jnp.float32
    q = q.astype(f32)
    k = k.astype(f32) * (dh ** -0.5)
    v = v.astype(f32)
    log_gamma = jnp.log(1.0 - 2.0 ** (-5.0 - jnp.arange(H, dtype=f32)))
    pos = jnp.arange(CHUNK, dtype=f32)
    rel = pos[:, None] - pos[None, :]
    dmask = jnp.where(rel[None] >= 0, jnp.exp(log_gamma[:, None, None] * jnp.maximum(rel, 0.0)[None]), 0.0)
    xi = jnp.exp(log_gamma[None, :] * (pos[:, None] + 1.0))
    zeta = jnp.exp(log_gamma[:, None] * (CHUNK - 1.0 - pos[None, :]))
    chunk_decay = jnp.exp(log_gamma * CHUNK)

    def step(R, inp):
        qj, kj, vj = inp
        s = jnp.einsum('blhd,bshd->bhls', qj, kj) * dmask[None]
        o = (jnp.einsum('bhls,bshe->blhe', s, vj)
             + jnp.einsum('blhd,bhde->blhe', qj, R) * xi[None, :, :, None])
        R_new = chunk_decay[None, :, None, None] * R + jnp.einsum('hs,bshd,bshe->bhde', zeta, kj, vj)
        return R_new, o

    R0 = jnp.zeros((B, H, dh, dh), f32)
    _, os_ = lax.scan(step, R0, (to_chunks(q), to_chunks(k), to_chunks(v)))
    return from_chunks(os_)


def diff_attention(q, k, v, lam):
    B, S, H, _, dh = q.shape
    f32 = jnp.float32
    q = q.astype(f32) * (dh ** -0.5)
    k = k.astype(f32)
    v = v.astype(f32)
    slopes = 2.0 ** (-8.0 * (jnp.arange(H, dtype=f32) + 1.0) / H)
    outs = []
    for blk in range(S // Q_BLOCK):
        q0 = blk * Q_BLOCK
        kend = q0 + Q_BLOCK
        qb = q[:, q0:kend]
        kb = k[:, :kend]
        vb = v[:, :kend]
        s = jnp.einsum('bqhcd,bkhcd->bhcqk', qb, kb)
        tq = jnp.arange(q0, kend)
        tk = jnp.arange(kend)
        dist = jnp.abs(tq[:, None] - tk[None, :]).astype(f32)
        allowed = (tk // CHUNK)[None, :] <= (tq // CHUNK)[:, None]
        bias = -slopes[:, None, None] * dist[None]
        s = jnp.where(allowed, s + bias[None, :, None], -jnp.inf)
        p = jax.nn.softmax(s, axis=-1)
        a = p[:, :, 0] - lam * p[:, :, 1]
        outs.append(jnp.einsum('bhqk,bkhe->bqhe', a, vb))
    return jnp.concatenate(outs, axis=1)


def split_in(proj):
    idx = np.cumsum(np.array(IN_SIZES))[:-1].tolist()
    return jnp.split(proj, idx, axis=-1)


def setup_inputs(seed: int = 0) -> dict:
    key = jax.random.key(seed)
    ks = jax.random.split(key, 16)
    f32 = jnp.float32

    def nrm(kk, shape, scale):
        return jax.random.normal(kk, shape, f32) * scale

    x = nrm(ks[0], (BATCH, SEQ, D_MODEL), 1.0)
    norm_gains = 1.0 + nrm(ks[1], (DEPTH, 4, D_MODEL), 0.05)
    w_in = nrm(ks[2], (DEPTH, D_MODEL, N_IN), D_MODEL ** -0.5)
    mlstm_conv = nrm(ks[3], (DEPTH, CONV_K, 2 * MLSTM_W), CONV_K ** -0.5)
    f_bias = jnp.broadcast_to(jnp.linspace(3.0, 6.0, MLSTM_HEADS, dtype=f32), (DEPTH, MLSTM_HEADS))
    i_bias = jnp.zeros((DEPTH, MLSTM_HEADS), f32)
    mlstm_gate_bias = jnp.stack([i_bias, f_bias], axis=1) + nrm(ks[4], (DEPTH, 2, MLSTM_HEADS), 0.1)
    diff_lambda = nrm(ks[5], (DEPTH, 4, DIFF_HD), 0.1)
    diff_subln = 1.0 + nrm(ks[6], (DEPTH, 2 * DIFF_HD), 0.05)
    w_mlstm_out = nrm(ks[7], (DEPTH, MLSTM_W, D_MODEL), MLSTM_W ** -0.5)
    w_diff_out = nrm(ks[8], (DEPTH, DIFF_W, D_MODEL), DIFF_W ** -0.5)
    w_ret_out = nrm(ks[9], (DEPTH, RET_W, D_MODEL), RET_W ** -0.5)
    w_out = nrm(ks[10], (DEPTH, D_MODEL, D_MODEL), D_MODEL ** -0.5)
    w_ffn_in = nrm(ks[11], (DEPTH, D_MODEL, 2 * FFN_HIDDEN), D_MODEL ** -0.5)
    w_ffn_out = nrm(ks[12], (DEPTH, FFN_HIDDEN, D_MODEL), FFN_HIDDEN ** -0.5)
    return {"x": x, "norm_gains": norm_gains, "w_in": w_in, "mlstm_conv": mlstm_conv,
            "mlstm_gate_bias": mlstm_gate_bias, "diff_lambda": diff_lambda,
            "diff_subln": diff_subln, "w_mlstm_out": w_mlstm_out, "w_diff_out": w_diff_out,
            "w_ret_out": w_ret_out, "w_out": w_out, "w_ffn_in": w_ffn_in,
            "w_ffn_out": w_ffn_out}


def reference(x, norm_gains, w_in, mlstm_conv, mlstm_gate_bias, diff_lambda, diff_subln,
              w_mlstm_out, w_diff_out, w_ret_out, w_out, w_ffn_in, w_ffn_out):
    B, S, _ = x.shape
    f32 = jnp.float32
    for l in range(DEPTH):
        g = norm_gains[l]
        h = rmsnorm(x, g[0])
        proj = h @ w_in[l]
        (mq, mk, mv, mo, mi, mf, dq, dk, dv, rq, rk, rv, rg, gates) = split_in(proj)

        qk = jax.nn.silu(causal_dwconv(jnp.concatenate([mq, mk], axis=-1), mlstm_conv[l]))
        mq_c, mk_c = jnp.split(qk, 2, axis=-1)
        hm = mlstm(mq_c.reshape(B, S, MLSTM_HEADS, MLSTM_HD),
                   mk_c.reshape(B, S, MLSTM_HEADS, MLSTM_HD),
                   mv.reshape(B, S, MLSTM_HEADS, MLSTM_HD),
                   mi + mlstm_gate_bias[l, 0],
                   mf + mlstm_gate_bias[l, 1])
        hm = jax.nn.sigmoid(mo.astype(f32)) * head_rms(hm).reshape(B, S, MLSTM_W)

        lam_init = 0.8 - 0.6 * math.exp(-0.3 * l)
        lmb = diff_lambda[l].astype(f32)
        lam = jnp.exp(jnp.sum(lmb[0] * lmb[1])) - jnp.exp(jnp.sum(lmb[2] * lmb[3])) + lam_init
        hd = diff_attention(dq.reshape(B, S, DIFF_HEADS, 2, DIFF_HD),
                            dk.reshape(B, S, DIFF_HEADS, 2, DIFF_HD),
                            dv.reshape(B, S, DIFF_HEADS, 2 * DIFF_HD), lam)
        hd = (rmsnorm(hd, diff_subln[l]) * (1.0 - lam_init)).reshape(B, S, DIFF_W)

        hr = retention(rq.reshape(B, S, RET_HEADS, RET_HD),
                       rk.reshape(B, S, RET_HEADS, RET_HD),
                       rv.reshape(B, S, RET_HEADS, RET_HD))
        hr = jax.nn.silu(rg.astype(f32)) * head_rms(hr).reshape(B, S, RET_W)

        gm, gd, gr = jnp.split(jax.nn.sigmoid(gates), 3, axis=-1)
        y = (gm * (hm.astype(x.dtype) @ w_mlstm_out[l])
             + gd * (hd.astype(x.dtype) @ w_diff_out[l])
             + gr * (hr.astype(x.dtype) @ w_ret_out[l]))
        y = y @ w_out[l]
        x = x + rmsnorm(y, g[1])

        h2 = rmsnorm(x, g[2])
        a, u = jnp.split(h2 @ w_ffn_in[l], 2, axis=-1)
        f = (jax.nn.silu(a) * u) @ w_ffn_out[l]
        x = x + rmsnorm(f, g[3])
    return x
```

```python
import functools
import math

import jax
import jax.numpy as jnp
from jax import lax
from jax.experimental import pallas as pl
from jax.experimental.pallas import tpu as pltpu

F32 = jnp.float32
BF16 = jnp.bfloat16

EPS = 1e-6
CHUNK = 64
D_MODEL = 1024
MLSTM_HEADS = 4
MLSTM_HD = 256
DIFF_HEADS = 8
DIFF_HD = 64
RET_HEADS = 4
RET_HD = 256
CONV_K = 4
FFN_HIDDEN = 2816
N_MAIN = 14 * D_MODEL
GATE_PAD = 128
NEG_BIG = -1e30

VMEM_LIMIT = 56 * 1024 * 1024


def _dot(a, b):
    return jnp.dot(a, b, preferred_element_type=F32)


def _dot_nt(a, b):
    return lax.dot_general(a, b, (((1,), (1,)), ((), ())), preferred_element_type=F32)


def _dot_tn(a, b):
    return lax.dot_general(a, b, (((0,), (0,)), ((), ())), preferred_element_type=F32)


def _sigmoid(x):
    return 1.0 / (1.0 + jnp.exp(-x))


def _rms(x, axis=-1):
    return x * lax.rsqrt(jnp.mean(x * x, axis=axis, keepdims=True) + EPS)


def _inproj_kernel(x_ref, g_ref, w_ref, wif_ref, o_ref, oif_ref, h_ref):
    @pl.when(pl.program_id(1) == 0)
    def _():
        x = x_ref[...]
        h = (_rms(x) * g_ref[...]).astype(BF16)
        h_ref[...] = h
        oif_ref[...] = _dot(h, wif_ref[...])

    o_ref[...] = _dot(h_ref[...], w_ref[...]).astype(BF16)


def _inproj(x2, gain, w_main, w_if, tm, tn):
    T, D = x2.shape
    n = w_main.shape[1]
    return pl.pallas_call(
        _inproj_kernel,
        grid=(T // tm, n // tn),
        in_specs=[
            pl.BlockSpec((tm, D), lambda i, j: (i, 0)),
            pl.BlockSpec((1, D), lambda i, j: (0, 0)),
            pl.BlockSpec((D, tn), lambda i, j: (0, j)),
            pl.BlockSpec((D, GATE_PAD), lambda i, j: (0, 0)),
        ],
        out_specs=[
            pl.BlockSpec((tm, tn), lambda i, j: (i, j)),
            pl.BlockSpec((tm, GATE_PAD), lambda i, j: (i, 0)),
        ],
        out_shape=[
            jax.ShapeDtypeStruct((T, n), BF16),
            jax.ShapeDtypeStruct((T, GATE_PAD), F32),
        ],
        scratch_shapes=[pltpu.VMEM((tm, D), BF16)],
        compiler_params=pltpu.CompilerParams(
            dimension_semantics=("parallel", "arbitrary"), vmem_limit_bytes=VMEM_LIMIT),
        name="inproj",
    )(x2, gain, w_main, w_if)


def _split3(x):
    a = x.astype(BF16)
    r = x - a.astype(F32)
    b = r.astype(BF16)
    c = (r - b.astype(F32)).astype(BF16)
    return a, b, c


def _mlstm_kernel(q_ref, k_ref, v_ref, o_ref, gt_ref, gb_ref, cw_ref, out_ref,
                  xq_ref, xk_ref, c_ref, n_ref, m_ref, *, lc):
    H, dh = MLSTM_HEADS, MLSTM_HD
    W = H * dh

    @pl.when(pl.program_id(1) == 0)
    def _():
        xq_ref[0:8, :] = jnp.zeros((8, W), F32)
        xk_ref[0:8, :] = jnp.zeros((8, W), F32)
        c_ref[...] = jnp.zeros_like(c_ref)
        n_ref[...] = jnp.zeros_like(n_ref)
        m_ref[...] = jnp.zeros_like(m_ref)

    def conv_silu(x_ref, st_ref, w):
        st_ref[8:8 + lc, :] = x_ref[...].astype(F32)
        z = st_ref[5:5 + lc, :] * w[0:1, :]
        z = z + st_ref[6:6 + lc, :] * w[1:2, :]
        z = z + st_ref[7:7 + lc, :] * w[2:3, :]
        z = z + st_ref[8:8 + lc, :] * w[3:4, :]
        tail = st_ref[lc:lc + 8, :]
        st_ref[0:8, :] = tail
        return z * _sigmoid(z)

    cw = cw_ref[...]
    qc = conv_silu(q_ref, xq_ref, cw[:, 0:W])
    kc = conv_silu(k_ref, xk_ref, cw[:, W:2 * W]) * (dh ** -0.5)

    g = gt_ref[...] + gb_ref[...]
    logf = jnp.minimum(g, 0.0) - jnp.log(1.0 + jnp.exp(-jnp.abs(g)))
    row = lax.broadcasted_iota(jnp.int32, (lc, lc), 0)
    col = lax.broadcasted_iota(jnp.int32, (lc, lc), 1)
    causal = row >= col
    tril = causal.astype(BF16)
    f1, f2, f3 = _split3(logf)
    bc = _dot(tril, f1) + _dot(tril, f2) + _dot(tril, f3)
    bt = bc.T
    gtr = g.T

    for h in range(H):
        sl = slice(h * dh, (h + 1) * dh)
        qh = qc[:, sl]
        kh = kc[:, sl]
        qb = qh.astype(BF16)
        kb = kh.astype(BF16)
        vb = v_ref[:, sl]
        b_col = bc[:, 4 + h:5 + h]
        i_col = g[:, h:h + 1]
        b_row = bt[4 + h:5 + h, :]
        i_row = gtr[h:h + 1, :]
        gsum = bc[lc - 1:lc, 4 + h:5 + h]
        m_old = m_ref[h:h + 1, 0:1]
        c_old = c_ref[h]
        n_old = n_ref[h:h + 1, :]

        d = jnp.where(causal, b_col - b_row + i_row, NEG_BIG)
        d_inter = b_col + m_old
        m_row = jnp.maximum(jnp.max(d, axis=-1, keepdims=True), d_inter)
        w_intra = jnp.exp(d - m_row)
        w_inter = jnp.exp(d_inter - m_row)
        s = _dot_nt(qb, kb) * w_intra
        num = _dot(s.astype(BF16), vb) + _dot(qb, c_old.astype(BF16)) * w_inter
        den = (jnp.sum(s, axis=-1, keepdims=True)
               + jnp.sum(qh * n_old, axis=-1, keepdims=True) * w_inter)
        den = jnp.maximum(jnp.abs(den), jnp.exp(-m_row))
        hh = num / den
        og = o_ref[:, sl].astype(F32)
        out_ref[:, sl] = (_sigmoid(og) * _rms(hh)).astype(BF16)

        a_col = gsum - b_col + i_col
        m_new = jnp.maximum(gsum + m_old, jnp.max(a_col, axis=0, keepdims=True))
        w_old = jnp.exp(gsum + m_old - m_new)
        kw = kh * jnp.exp(a_col - m_new)
        c_ref[h] = w_old * c_old + _dot_tn(kw.astype(BF16), vb)
        n_ref[h:h + 1, :] = w_old * n_old + jnp.sum(kw, axis=0, keepdims=True)
        m_ref[h:h + 1, :] = jnp.broadcast_to(m_new, (1, 128))


def _mlstm(proj, gates, gate_bias, conv_w, B, S, lc):
    T = B * S
    nc = S // lc
    W = MLSTM_HEADS * MLSTM_HD
    blk = lambda c0: pl.BlockSpec((lc, W), lambda b, c: (b * nc + c, c0))
    return pl.pallas_call(
        functools.partial(_mlstm_kernel, lc=lc),
        grid=(B, nc),
        in_specs=[
            blk(0), blk(1), blk(2), blk(3),
            pl.BlockSpec((lc, GATE_PAD), lambda b, c: (b * nc + c, 0)),
            pl.BlockSpec((1, GATE_PAD), lambda b, c: (0, 0)),
            pl.BlockSpec((CONV_K, 2 * W), lambda b, c: (0, 0)),
        ],
        out_specs=pl.BlockSpec((lc, W), lambda b, c: (b * nc + c, 0)),
        out_shape=jax.ShapeDtypeStruct((T, W), BF16),
        scratch_shapes=[
            pltpu.VMEM((lc + 8, W), F32),
            pltpu.VMEM((lc + 8, W), F32),
            pltpu.VMEM((MLSTM_HEADS, MLSTM_HD, MLSTM_HD), F32),
            pltpu.VMEM((8, MLSTM_HD), F32),
            pltpu.VMEM((8, 128), F32),
        ],
        compiler_params=pltpu.CompilerParams(
            dimension_semantics=("arbitrary", "arbitrary"), vmem_limit_bytes=VMEM_LIMIT),
        name="mlstm",
    )(proj, proj, proj, proj, gates, gate_bias, conv_w)


def _ret_kernel(q_ref, k_ref, v_ref, g_ref, out_ref, r_ref, dm_ref, *, lc):
    H, dh = RET_HEADS, RET_HD

    @pl.when(pl.program_id(1) == 0)
    def _():
        r_ref[...] = jnp.zeros_like(r_ref)

    @pl.when((pl.program_id(0) == 0) & (pl.program_id(1) == 0))
    def _():
        row = lax.broadcasted_iota(jnp.int32, (lc, lc), 0)
        col = lax.broadcasted_iota(jnp.int32, (lc, lc), 1)
        rel = (row - col).astype(F32)
        for h in range(H):
            lg = math.log(1.0 - 2.0 ** (-5.0 - h))
            dm_ref[h] = jnp.where(rel >= 0.0, jnp.exp(lg * jnp.maximum(rel, 0.0)), 0.0)

    pos = lax.broadcasted_iota(jnp.int32, (lc, 1), 0).astype(F32)
    for h in range(H):
        lg = math.log(1.0 - 2.0 ** (-5.0 - h))
        sl = slice(h * dh, (h + 1) * dh)
        qb = q_ref[:, sl]
        kf = k_ref[:, sl].astype(F32) * (dh ** -0.5)
        vb = v_ref[:, sl]
        r_old = r_ref[h]
        xi = jnp.exp(lg * (pos + 1.0))
        zeta = jnp.exp(lg * (lc - 1.0 - pos))
        s = _dot_nt(qb, kf.astype(BF16)) * dm_ref[h]
        o = _dot(s.astype(BF16), vb) + _dot(qb, r_old.astype(BF16)) * xi
        r_ref[h] = math.exp(lg * lc) * r_old + _dot_tn((kf * zeta).astype(BF16), vb)
        gg = g_ref[:, sl].astype(F32)
        out_ref[:, sl] = (gg * _sigmoid(gg) * _rms(o)).astype(BF16)


def _retention(proj, B, S, lc):
    T = B * S
    nc = S // lc
    W = RET_HEADS * RET_HD
    blk = lambda c0: pl.BlockSpec((lc, W), lambda b, c: (b * nc + c, c0))
    return pl.pallas_call(
        functools.partial(_ret_kernel, lc=lc),
        grid=(B, nc),
        in_specs=[blk(7), blk(8), blk(9), blk(10)],
        out_specs=pl.BlockSpec((lc, W), lambda b, c: (b * nc + c, 0)),
        out_shape=jax.ShapeDtypeStruct((T, W), BF16),
        scratch_shapes=[
            pltpu.VMEM((RET_HEADS, RET_HD, RET_HD), F32),
            pltpu.VMEM((RET_HEADS, lc, lc), F32),
        ],
        compiler_params=pltpu.CompilerParams(
            dimension_semantics=("arbitrary", "arbitrary"), vmem_limit_bytes=VMEM_LIMIT),
        name="retention",
    )(proj, proj, proj, proj)


def _dattn_kernel(q_ref, k_ref, v_ref, lam_ref, sub_ref, out_ref,
                  k1_ref, k2_ref, cd_ref, acc_ref, m_ref, l_ref, *, bq, lam_init):
    dh = DIFF_HD
    S = k_ref.shape[0]
    h = pl.program_id(1)
    qi = pl.program_id(2)
    slope = lax.bitcast_convert_type(jnp.full((1, 1), (126 - h) << 23, jnp.int32), F32)

    @pl.when(qi == 0)
    def _():
        lane = lax.broadcasted_iota(jnp.int32, (bq, 128), 1)
        rloc = lax.broadcasted_iota(jnp.int32, (bq, 128), 0)

        def build(i, carry):
            r0 = pl.multiple_of(i * bq, bq)
            t = rloc + r0
            feat = jnp.where(lane == dh, slope * (t - (t & (CHUNK - 1))).astype(F32),
                             jnp.where(lane == dh + 1, slope * (t & (CHUNK - 1)).astype(F32),
                                       jnp.where(lane == dh + 2, 1.0, 0.0)))
            kf = k_ref[pl.ds(r0, bq), :].astype(F32)
            k1_ref[pl.ds(r0, bq), :] = jnp.where(lane < dh, kf, feat).astype(BF16)
            k2_ref[pl.ds(r0, bq), :] = jnp.where(lane < dh, pltpu.roll(kf, dh, 1), feat).astype(BF16)
            return carry

        lax.fori_loop(0, S // bq, build, 0)
        i2 = lax.broadcasted_iota(jnp.int32, (bq, bq), 0)
        j2 = lax.broadcasted_iota(jnp.int32, (bq, bq), 1)
        fut = jnp.where(j2 > i2, (-2.0 * slope) * (j2 - i2).astype(F32), 0.0)
        cd_ref[...] = jnp.where((j2 >> 6) <= (i2 >> 6), fut, NEG_BIG)

    lane = lax.broadcasted_iota(jnp.int32, (bq, 128), 1)
    t0f = (qi * bq).astype(F32)
    qfeat = jnp.where((lane == dh) | (lane == dh + 1), 1.0,
                      jnp.where(lane == dh + 2, -slope * t0f, 0.0))
    qf = q_ref[...].astype(F32) * (dh ** -0.5)
    qa = (jnp.where(lane < dh, qf, qfeat).astype(BF16),
          jnp.where(lane < dh, pltpu.roll(qf, dh, 1), qfeat).astype(BF16))
    kas = (k1_ref, k2_ref)

    r0 = pl.multiple_of(qi * bq, bq)
    vd = v_ref[pl.ds(r0, bq), :]
    for c in range(2):
        s = _dot_nt(qa[c], kas[c][pl.ds(r0, bq), :]) + cd_ref[...]
        m = jnp.max(s, axis=-1, keepdims=True)
        p = jnp.exp(s - m)
        m_ref[c] = m
        l_ref[c] = jnp.sum(p, axis=-1, keepdims=True)
        acc_ref[c] = _dot(p.astype(BF16), vd)

    def body(kb, carry):
        k0 = pl.multiple_of(kb * bq, bq)
        vb = v_ref[pl.ds(k0, bq), :]
        for c in range(2):
            s = _dot_nt(qa[c], kas[c][pl.ds(k0, bq), :])
            m_old = m_ref[c]
            m_new = jnp.maximum(m_old, jnp.max(s, axis=-1, keepdims=True))
            alpha = jnp.exp(m_old - m_new)
            p = jnp.exp(s - m_new)
            l_ref[c] = alpha * l_ref[c] + jnp.sum(p, axis=-1, keepdims=True)
            acc_ref[c] = alpha * acc_ref[c] + _dot(p.astype(BF16), vb)
            m_ref[c] = m_new
        return carry

    lax.fori_loop(0, qi, body, 0)

    lm = lam_ref[...]
    lam = (jnp.exp(jnp.sum(lm[0:1, :] * lm[1:2, :], axis=-1, keepdims=True))
           - jnp.exp(jnp.sum(lm[2:3, :] * lm[3:4, :], axis=-1, keepdims=True)) + lam_init)
    o = acc_ref[0] / l_ref[0] - lam * (acc_ref[1] / l_ref[1])
    out_ref[...] = ((_rms(o) * sub_ref[...]) * (1.0 - lam_init)).astype(BF16)


def _dattn(proj, lam_p, subln, B, S, bq, lam_init):
    T = B * S
    nq = S // bq
    W = DIFF_HEADS * 2 * DIFF_HD
    return pl.pallas_call(
        functools.partial(_dattn_kernel, bq=bq, lam_init=lam_init),
        grid=(B, DIFF_HEADS, nq),
        in_specs=[
            pl.BlockSpec((bq, 128), lambda b, h, i: (b * nq + i, 32 + h)),
            pl.BlockSpec((S, 128), lambda b, h, i: (b, 40 + h)),
            pl.BlockSpec((S, 128), lambda b, h, i: (b, 48 + h)),
            pl.BlockSpec((4, DIFF_HD), lambda b, h, i: (0, 0)),
            pl.BlockSpec((1, 128), lambda b, h, i: (0, 0)),
        ],
        out_specs=pl.BlockSpec((bq, 128), lambda b, h, i: (b * nq + i, h)),
        out_shape=jax.ShapeDtypeStruct((T, W), BF16),
        scratch_shapes=[
            pltpu.VMEM((S, 128), BF16),
            pltpu.VMEM((S, 128), BF16),
            pltpu.VMEM((bq, bq), F32),
            pltpu.VMEM((2, bq, 128), F32),
            pltpu.VMEM((2, bq, 1), F32),
            pltpu.VMEM((2, bq, 1), F32),
        ],
        compiler_params=pltpu.CompilerParams(
            dimension_semantics=("arbitrary", "arbitrary", "arbitrary"),
            vmem_limit_bytes=VMEM_LIMIT),
        name="dattn",
    )(proj, proj, proj, lam_p, subln)


def _merge_kernel(hm_ref, hd_ref, hr_ref, gm_ref, gd_ref, gr_ref, x_ref,
                  wm_ref, wd_ref, wr_ref, wo_ref, g_ref, out_ref):
    y = _sigmoid(gm_ref[...].astype(F32)) * _dot(hm_ref[...], wm_ref[...])
    y = y + _sigmoid(gd_ref[...].astype(F32)) * _dot(hd_ref[...], wd_ref[...])
    y = y + _sigmoid(gr_ref[...].astype(F32)) * _dot(hr_ref[...], wr_ref[...])
    y2 = _dot(y.astype(BF16), wo_ref[...])
    out_ref[...] = x_ref[...] + _rms(y2) * g_ref[...]


def _merge(hm, hd, hr, proj, x2, wm, wd, wr, wo, gain, tm):
    T, D = x2.shape
    row = lambda c0: pl.BlockSpec((tm, D), lambda i: (i, c0))
    wspec = pl.BlockSpec((D, D), lambda i: (0, 0))
    return pl.pallas_call(
        _merge_kernel,
        grid=(T // tm,),
        in_specs=[row(0), row(0), row(0), row(11), row(12), row(13), row(0),
                  wspec, wspec, wspec, wspec, pl.BlockSpec((1, D), lambda i: (0, 0))],
        out_specs=row(0),
        out_shape=jax.ShapeDtypeStruct((T, D), F32),
        compiler_params=pltpu.CompilerParams(
            dimension_semantics=("parallel",), vmem_limit_bytes=VMEM_LIMIT),
        name="merge",
    )(hm, hd, hr, proj, proj, proj, x2, wm, wd, wr, wo, gain)


def _ffn_kernel(x_ref, g2_ref, g3_ref, wa_ref, wu_ref, wo_ref, out_ref, *, hc):
    x = x_ref[...]
    h = (_rms(x) * g2_ref[...]).astype(BF16)
    f = None
    for c in range(FFN_HIDDEN // hc):
        sl = slice(c * hc, (c + 1) * hc)
        a = _dot(h, wa_ref[:, sl])
        u = _dot(h, wu_ref[:, sl])
        t = ((a * _sigmoid(a)) * u).astype(BF16)
        part = _dot(t, wo_ref[sl, :])
        f = part if f is None else f + part
    out_ref[...] = x + _rms(f) * g3_ref[...]


def _ffn(x2, g2, g3, wa, wu, wo, tm, hc):
    T, D = x2.shape
    Hd = wa.shape[1]
    vec = pl.BlockSpec((1, D), lambda i: (0, 0))
    return pl.pallas_call(
        functools.partial(_ffn_kernel, hc=hc),
        grid=(T // tm,),
        in_specs=[pl.BlockSpec((tm, D), lambda i: (i, 0)), vec, vec,
                  pl.BlockSpec((D, Hd), lambda i: (0, 0)),
                  pl.BlockSpec((D, Hd), lambda i: (0, 0)),
                  pl.BlockSpec((Hd, D), lambda i: (0, 0))],
        out_specs=pl.BlockSpec((tm, D), lambda i: (i, 0)),
        out_shape=jax.ShapeDtypeStruct((T, D), F32),
        compiler_params=pltpu.CompilerParams(
            dimension_semantics=("parallel",), vmem_limit_bytes=VMEM_LIMIT),
        name="ffn",
    )(x2, g2, g3, wa, wu, wo)


def _tiles(B, S):
    T = B * S
    return dict(
        tm_in=min(2048, T), tn_in=1024,
        lc=min(256, S), bq=min(512, S),
        tm_merge=min(512, T), tm_ffn=min(512, T), hc=1408,
    )


def kernel(x, norm_gains, w_in, mlstm_conv, mlstm_gate_bias, diff_lambda, diff_subln,
           w_mlstm_out, w_diff_out, w_ret_out, w_out, w_ffn_in, w_ffn_out):
    B, S, D = x.shape
    depth = w_in.shape[0]
    T = B * S
    cfg = _tiles(B, S)
    x2 = x.reshape(T, D).astype(F32)
    W = MLSTM_HEADS * MLSTM_HD

    for l in range(depth):
        g = norm_gains[l].astype(F32)
        wl = w_in[l]
        w_main = jnp.concatenate([wl[:, :4 * W], wl[:, 4 * W + 8:]], axis=1).astype(BF16)
        w_if = jnp.pad(wl[:, 4 * W:4 * W + 8], ((0, 0), (0, GATE_PAD - 8))).astype(BF16)
        proj, gates = _inproj(x2, g[0:1], w_main, w_if, cfg["tm_in"], cfg["tn_in"])

        gate_bias = jnp.pad(mlstm_gate_bias[l].astype(F32).reshape(1, 8), ((0, 0), (0, GATE_PAD - 8)))
        hm = _mlstm(proj, gates, gate_bias, mlstm_conv[l].astype(F32), B, S, cfg["lc"])

        lam_init = 0.8 - 0.6 * math.exp(-0.3 * l)
        hd = _dattn(proj, diff_lambda[l].astype(F32), diff_subln[l].astype(F32).reshape(1, 128),
                    B, S, cfg["bq"], lam_init)

        hr = _retention(proj, B, S, cfg["lc"])

        x2 = _merge(hm, hd, hr, proj, x2,
                    w_mlstm_out[l].astype(BF16), w_diff_out[l].astype(BF16),
                    w_ret_out[l].astype(BF16), w_out[l].astype(BF16), g[1:2], cfg["tm_merge"])

        wf = w_ffn_in[l]
        x2 = _ffn(x2, g[2:3], g[3:4], wf[:, :FFN_HIDDEN].astype(BF16), wf[:, FFN_HIDDEN:].astype(BF16),
                  w_ffn_out[l].astype(BF16), cfg["tm_ffn"], cfg["hc"])

    return x2.reshape(B, S, D).astype(x.dtype)
```

```python
import functools
import math

import jax
import jax.numpy as jnp
from jax import lax
from jax.experimental import pallas as pl
from jax.experimental.pallas import tpu as pltpu

F32 = jnp.float32
BF16 = jnp.bfloat16

EPS = 1e-6
CHUNK = 64
D_MODEL = 1024
MLSTM_HEADS = 4
MLSTM_HD = 256
DIFF_HEADS = 8
DIFF_HD = 64
RET_HEADS = 4
RET_HD = 256
CONV_K = 4
FFN_HIDDEN = 2816
N_MAIN = 14 * D_MODEL
GATE_PAD = 128
NEG_BIG = -1e30

VMEM_LIMIT = 56 * 1024 * 1024


def _dot(a, b):
    return jnp.dot(a, b, preferred_element_type=F32)


def _dot_nt(a, b):
    return lax.dot_general(a, b, (((1,), (1,)), ((), ())), preferred_element_type=F32)


def _dot_tn(a, b):
    return lax.dot_general(a, b, (((0,), (0,)), ((), ())), preferred_element_type=F32)


def _sigmoid(x):
    return 1.0 / (1.0 + jnp.exp(-x))


def _rms(x, axis=-1):
    return x * lax.rsqrt(jnp.mean(x * x, axis=axis, keepdims=True) + EPS)


def _inproj_kernel(x_ref, g_ref, w_ref, wif_ref, o_ref, oif_ref, h_ref):
    @pl.when(pl.program_id(1) == 0)
    def _():
        x = x_ref[...]
        h = (_rms(x) * g_ref[...]).astype(BF16)
        h_ref[...] = h
        oif_ref[...] = _dot(h, wif_ref[...])

    o_ref[...] = _dot(h_ref[...], w_ref[...]).astype(BF16)


def _inproj(x2, gain, w_main, w_if, tm, tn):
    T, D = x2.shape
    n = w_main.shape[1]
    return pl.pallas_call(
        _inproj_kernel,
        grid=(T // tm, n // tn),
        in_specs=[
            pl.BlockSpec((tm, D), lambda i, j: (i, 0)),
            pl.BlockSpec((1, D), lambda i, j: (0, 0)),
            pl.BlockSpec((D, tn), lambda i, j: (0, j)),
            pl.BlockSpec((D, GATE_PAD), lambda i, j: (0, 0)),
        ],
        out_specs=[
            pl.BlockSpec((tm, tn), lambda i, j: (i, j)),
            pl.BlockSpec((tm, GATE_PAD), lambda i, j: (i, 0)),
        ],
        out_shape=[
            jax.ShapeDtypeStruct((T, n), BF16),
            jax.ShapeDtypeStruct((T, GATE_PAD), F32),
        ],
        scratch_shapes=[pltpu.VMEM((tm, D), BF16)],
        compiler_params=pltpu.CompilerParams(
            dimension_semantics=("parallel", "arbitrary"), vmem_limit_bytes=VMEM_LIMIT),
        name="inproj",
    )(x2, gain, w_main, w_if)


def _split3(x):
    a = x.astype(BF16)
    r = x - a.astype(F32)
    b = r.astype(BF16)
    c = (r - b.astype(F32)).astype(BF16)
    return a, b, c


def _mlstm_kernel(q_ref, k_ref, v_ref, o_ref, gt_ref, gb_ref, cw_ref, out_ref,
                  xq_ref, xk_ref, c_ref, n_ref, m_ref, *, lc):
    H, dh = MLSTM_HEADS, MLSTM_HD
    W = H * dh

    @pl.when(pl.program_id(1) == 0)
    def _():
        xq_ref[0:8, :] = jnp.zeros((8, W), F32)
        xk_ref[0:8, :] = jnp.zeros((8, W), F32)
        c_ref[...] = jnp.zeros_like(c_ref)
        n_ref[...] = jnp.zeros_like(n_ref)
        m_ref[...] = jnp.zeros_like(m_ref)

    def conv_silu(x_ref, st_ref, w):
        st_ref[8:8 + lc, :] = x_ref[...].astype(F32)
        z = st_ref[5:5 + lc, :] * w[0:1, :]
        z = z + st_ref[6:6 + lc, :] * w[1:2, :]
        z = z + st_ref[7:7 + lc, :] * w[2:3, :]
        z = z + st_ref[8:8 + lc, :] * w[3:4, :]
        tail = st_ref[lc:lc + 8, :]
        st_ref[0:8, :] = tail
        return z * _sigmoid(z)

    cw = cw_ref[...]
    qc = conv_silu(q_ref, xq_ref, cw[:, 0:W])
    kc = conv_silu(k_ref, xk_ref, cw[:, W:2 * W]) * (dh ** -0.5)

    g = gt_ref[...] + gb_ref[...]
    logf = jnp.minimum(g, 0.0) - jnp.log(1.0 + jnp.exp(-jnp.abs(g)))
    row = lax.broadcasted_iota(jnp.int32, (lc, lc), 0)
    col = lax.broadcasted_iota(jnp.int32, (lc, lc), 1)
    causal = row >= col
    tril = causal.astype(BF16)
    f1, f2, f3 = _split3(logf)
    bc = _dot(tril, f1) + _dot(tril, f2) + _dot(tril, f3)
    bt = bc.T
    gtr = g.T

    for h in range(H):
        sl = slice(h * dh, (h + 1) * dh)
        qh = qc[:, sl]
        kh = kc[:, sl]
        qb = qh.astype(BF16)
        kb = kh.astype(BF16)
        vb = v_ref[:, sl]
        b_col = bc[:, 4 + h:5 + h]
        i_col = g[:, h:h + 1]
        b_row = bt[4 + h:5 + h, :]
        i_row = gtr[h:h + 1, :]
        gsum = bc[lc - 1:lc, 4 + h:5 + h]
        m_old = m_ref[h:h + 1, 0:1]
        c_old = c_ref[h]
        n_old = n_ref[h:h + 1, :]

        d = jnp.where(causal, b_col - b_row + i_row, NEG_BIG)
        d_inter = b_col + m_old
        m_row = jnp.maximum(jnp.max(d, axis=-1, keepdims=True), d_inter)
        w_intra = jnp.exp(d - m_row)
        w_inter = jnp.exp(d_inter - m_row)
        s = _dot_nt(qb, kb) * w_intra
        num = _dot(s.astype(BF16), vb) + _dot(qb, c_old.astype(BF16)) * w_inter
        den = (jnp.sum(s, axis=-1, keepdims=True)
               + jnp.sum(qh * n_old, axis=-1, keepdims=True) * w_inter)
        den = jnp.maximum(jnp.abs(den), jnp.exp(-m_row))
        hh = num / den
        og = o_ref[:, sl].astype(F32)
        out_ref[:, sl] = (_sigmoid(og) * _rms(hh)).astype(BF16)

        a_col = gsum - b_col + i_col
        m_new = jnp.maximum(gsum + m_old, jnp.max(a_col, axis=0, keepdims=True))
        w_old = jnp.exp(gsum + m_old - m_new)
        kw = kh * jnp.exp(a_col - m_new)
        c_ref[h] = w_old * c_old + _dot_tn(kw.astype(BF16), vb)
        n_ref[h:h + 1, :] = w_old * n_old + jnp.sum(kw, axis=0, keepdims=True)
        m_ref[h:h + 1, :] = jnp.broadcast_to(m_new, (1, 128))


def _mlstm(proj, gates, gate_bias, conv_w, B, S, lc):
    T = B * S
    nc = S // lc
    W = MLSTM_HEADS * MLSTM_HD
    blk = lambda c0: pl.BlockSpec((lc, W), lambda b, c: (b * nc + c, c0))
    return pl.pallas_call(
        functools.partial(_mlstm_kernel, lc=lc),
        grid=(B, nc),
        in_specs=[
            blk(0), blk(1), blk(2), blk(3),
            pl.BlockSpec((lc, GATE_PAD), lambda b, c: (b * nc + c, 0)),
            pl.BlockSpec((1, GATE_PAD), lambda b, c: (0, 0)),
            pl.BlockSpec((CONV_K, 2 * W), lambda b, c: (0, 0)),
        ],
        out_specs=pl.BlockSpec((lc, W), lambda b, c: (b * nc + c, 0)),
        out_shape=jax.ShapeDtypeStruct((T, W), BF16),
        scratch_shapes=[
            pltpu.VMEM((lc + 8, W), F32),
            pltpu.VMEM((lc + 8, W), F32),
            pltpu.VMEM((MLSTM_HEADS, MLSTM_HD, MLSTM_HD), F32),
            pltpu.VMEM((8, MLSTM_HD), F32),
            pltpu.VMEM((8, 128), F32),
        ],
        compiler_params=pltpu.CompilerParams(
            dimension_semantics=("arbitrary", "arbitrary"), vmem_limit_bytes=VMEM_LIMIT),
        name="mlstm",
    )(proj, proj, proj, proj, gates, gate_bias, conv_w)


def _ret_kernel(q_ref, k_ref, v_ref, g_ref, out_ref, r_ref, dm_ref, *, lc):
    H, dh = RET_HEADS, RET_HD

    @pl.when(pl.program_id(1) == 0)
    def _():
        r_ref[...] = jnp.zeros_like(r_ref)

    @pl.when((pl.program_id(0) == 0) & (pl.program_id(1) == 0))
    def _():
        row = lax.broadcasted_iota(jnp.int32, (lc, lc), 0)
        col = lax.broadcasted_iota(jnp.int32, (lc, lc), 1)
        rel = (row - col).astype(F32)
        for h in range(H):
            lg = math.log(1.0 - 2.0 ** (-5.0 - h))
            dm_ref[h] = jnp.where(rel >= 0.0, jnp.exp(lg * jnp.maximum(rel, 0.0)), 0.0)

    pos = lax.broadcasted_iota(jnp.int32, (lc, 1), 0).astype(F32)
    for h in range(H):
        lg = math.log(1.0 - 2.0 ** (-5.0 - h))
        sl = slice(h * dh, (h + 1) * dh)
        qb = q_ref[:, sl]
        kf = k_ref[:, sl].astype(F32) * (dh ** -0.5)
        vb = v_ref[:, sl]
        r_old = r_ref[h]
        xi = jnp.exp(lg * (pos + 1.0))
        zeta = jnp.exp(lg * (lc - 1.0 - pos))
        s = _dot_nt(qb, kf.astype(BF16)) * dm_ref[h]
        o = _dot(s.astype(BF16), vb) + _dot(qb, r_old.astype(BF16)) * xi
        r_ref[h] = math.exp(lg * lc) * r_old + _dot_tn((kf * zeta).astype(BF16), vb)
        gg = g_ref[:, sl].astype(F32)
        out_ref[:, sl] = (gg * _sigmoid(gg) * _rms(o)).astype(BF16)


def _retention(proj, B, S, lc):
    T = B * S
    nc = S // lc
    W = RET_HEADS * RET_HD
    blk = lambda c0: pl.BlockSpec((lc, W), lambda b, c: (b * nc + c, c0))
    return pl.pallas_call(
        functools.partial(_ret_kernel, lc=lc),
        grid=(B, nc),
        in_specs=[blk(7), blk(8), blk(9), blk(10)],
        out_specs=pl.BlockSpec((lc, W), lambda b, c: (b * nc + c, 0)),
        out_shape=jax.ShapeDtypeStruct((T, W), BF16),
        scratch_shapes=[
            pltpu.VMEM((RET_HEADS, RET_HD, RET_HD), F32),
            pltpu.VMEM((RET_HEADS, lc, lc), F32),
        ],
        compiler_params=pltpu.CompilerParams(
            dimension_semantics=("arbitrary", "arbitrary"), vmem_limit_bytes=VMEM_LIMIT),
        name="retention",
    )(proj, proj, proj, proj)


def _dattn_kernel(q_ref, k_ref, v_ref, lam_ref, sub_ref, out_ref,
                  k1_ref, k2_ref, vp_ref, cd_ref, acc_ref, m_ref, qa_ref, s_ref, *, bq, lam_init):
    dh = DIFF_HD
    S = k_ref.shape[0]
    h = pl.program_id(1)
    qi = pl.program_id(2)
    slope = lax.bitcast_convert_type(jnp.full((1, 1), (126 - h) << 23, jnp.int32), F32)

    @pl.when(qi == 0)
    def _():
        lane = lax.broadcasted_iota(jnp.int32, (bq, 128), 1)
        rloc = lax.broadcasted_iota(jnp.int32, (bq, 128), 0)

        def build(i, carry):
            r0 = pl.multiple_of(i * bq, bq)
            t = rloc + r0
            feat = jnp.where(lane == dh, slope * (t - (t & (CHUNK - 1))).astype(F32),
                             jnp.where(lane == dh + 1, slope * (t & (CHUNK - 1)).astype(F32),
                                       jnp.where(lane == dh + 2, 1.0, 0.0)))
            kf = k_ref[pl.ds(r0, bq), :].astype(F32)
            k1_ref[pl.ds(r0, bq), :] = jnp.where(lane < dh, kf, feat).astype(BF16)
            k2_ref[pl.ds(r0, bq), :] = jnp.where(lane < dh, pltpu.roll(kf, dh, 1), feat).astype(BF16)
            vp_ref[pl.ds(r0, bq), 0:128] = v_ref[pl.ds(r0, bq), :]
            vp_ref[pl.ds(r0, bq), 128:256] = jnp.ones((bq, 128), BF16)
            return carry

        lax.fori_loop(0, S // bq, build, 0)
        i2 = lax.broadcasted_iota(jnp.int32, (bq, bq), 0)
        j2 = lax.broadcasted_iota(jnp.int32, (bq, bq), 1)
        fut = jnp.where(j2 > i2, (-2.0 * slope) * (j2 - i2).astype(F32), 0.0)
        cd_ref[...] = jnp.where((j2 >> 6) <= (i2 >> 6), fut, NEG_BIG)

    lane = lax.broadcasted_iota(jnp.int32, (bq, 128), 1)
    t0f = (qi * bq).astype(F32)
    qfeat = jnp.where((lane == dh) | (lane == dh + 1), 1.0,
                      jnp.where(lane == dh + 2, -slope * t0f, 0.0))
    qf = q_ref[...].astype(F32) * (dh ** -0.5)
    qa_ref[0] = jnp.where(lane < dh, qf, qfeat).astype(BF16)
    qa_ref[1] = jnp.where(lane < dh, pltpu.roll(qf, dh, 1), qfeat).astype(BF16)
    kas = (k1_ref, k2_ref)

    m_ref[...] = jnp.full(m_ref.shape, NEG_BIG, F32)
    acc_ref[...] = jnp.zeros(acc_ref.shape, F32)

    def scores(blk, diag, slot):
        k0 = pl.multiple_of(blk * bq, bq)
        for c in range(2):
            sc = _dot_nt(qa_ref[c], kas[c][pl.ds(k0, bq), :])
            s_ref[slot, c] = sc + cd_ref[...] if diag else sc

    def update(blk, slot):
        vb = vp_ref[pl.ds(pl.multiple_of(blk * bq, bq), bq), :]
        for c in range(2):
            s = s_ref[slot, c]
            m_prev = m_ref[c]
            m_next = jnp.maximum(m_prev, jnp.max(s, axis=-1, keepdims=True))
            p = jnp.exp(s - jnp.concatenate([m_next] * (bq // 128), axis=1))
            alpha = jnp.exp(m_prev - m_next)
            acc_ref[c] = jnp.concatenate([alpha, alpha], axis=1) * acc_ref[c] + _dot(p.astype(BF16), vb)
            m_ref[c] = m_next

    last = jnp.maximum(qi - 1, 0)
    scores(qi, True, 0)

    def body(pr, carry):
        t = 2 * pr
        scores(jnp.minimum(t, last), False, 1)
        update(jnp.where(t == 0, qi, t - 1), 0)

        @pl.when(t + 1 <= qi)
        def _():
            scores(jnp.minimum(t + 1, last), False, 0)
            update(t, 1)

        return carry

    lax.fori_loop(0, (qi + 2) // 2, body, 0)

    lm = lam_ref[...]
    lam = (jnp.exp(jnp.sum(lm[0:1, :] * lm[1:2, :], axis=-1, keepdims=True))
           - jnp.exp(jnp.sum(lm[2:3, :] * lm[3:4, :], axis=-1, keepdims=True)) + lam_init)
    a0 = acc_ref[0]
    a1 = acc_ref[1]
    o = a0[:, 0:128] / a0[:, 128:256] - lam * (a1[:, 0:128] / a1[:, 128:256])
    out_ref[...] = ((_rms(o) * sub_ref[...]) * (1.0 - lam_init)).astype(BF16)


def _dattn(proj, lam_p, subln, B, S, bq, lam_init):
    T = B * S
    nq = S // bq
    W = DIFF_HEADS * 2 * DIFF_HD
    return pl.pallas_call(
        functools.partial(_dattn_kernel, bq=bq, lam_init=lam_init),
        grid=(B, DIFF_HEADS, nq),
        in_specs=[
            pl.BlockSpec((bq, 128), lambda b, h, i: (b * nq + i, 32 + h)),
            pl.BlockSpec((S, 128), lambda b, h, i: (b, 40 + h)),
            pl.BlockSpec((S, 128), lambda b, h, i: (b, 48 + h)),
            pl.BlockSpec((4, DIFF_HD), lambda b, h, i: (0, 0)),
            pl.BlockSpec((1, 128), lambda b, h, i: (0, 0)),
        ],
        out_specs=pl.BlockSpec((bq, 128), lambda b, h, i: (b * nq + i, h)),
        out_shape=jax.ShapeDtypeStruct((T, W), BF16),
        scratch_shapes=[
            pltpu.VMEM((S, 128), BF16),
            pltpu.VMEM((S, 128), BF16),
            pltpu.VMEM((S, 256), BF16),
            pltpu.VMEM((bq, bq), F32),
            pltpu.VMEM((2, bq, 256), F32),
            pltpu.VMEM((2, bq, 128), F32),
            pltpu.VMEM((2, bq, 128), BF16),
            pltpu.VMEM((2, 2, bq, bq), F32),
        ],
        compiler_params=pltpu.CompilerParams(
            dimension_semantics=("arbitrary", "arbitrary", "arbitrary"),
            vmem_limit_bytes=VMEM_LIMIT),
        name="dattn",
    )(proj, proj, proj, lam_p, subln)


def _merge_kernel(hm_ref, hd_ref, hr_ref, gm_ref, gd_ref, gr_ref, x_ref,
                  wm_ref, wd_ref, wr_ref, wo_ref, g_ref, out_ref):
    y = _sigmoid(gm_ref[...].astype(F32)) * _dot(hm_ref[...], wm_ref[...])
    y = y + _sigmoid(gd_ref[...].astype(F32)) * _dot(hd_ref[...], wd_ref[...])
    y = y + _sigmoid(gr_ref[...].astype(F32)) * _dot(hr_ref[...], wr_ref[...])
    y2 = _dot(y.astype(BF16), wo_ref[...])
    out_ref[...] = x_ref[...] + _rms(y2) * g_ref[...]


def _merge(hm, hd, hr, proj, x2, wm, wd, wr, wo, gain, tm):
    T, D = x2.shape
    row = lambda c0: pl.BlockSpec((tm, D), lambda i: (i, c0))
    wspec = pl.BlockSpec((D, D), lambda i: (0, 0))
    return pl.pallas_call(
        _merge_kernel,
        grid=(T // tm,),
        in_specs=[row(0), row(0), row(0), row(11), row(12), row(13), row(0),
                  wspec, wspec, wspec, wspec, pl.BlockSpec((1, D), lambda i: (0, 0))],
        out_specs=row(0),
        out_shape=jax.ShapeDtypeStruct((T, D), F32),
        compiler_params=pltpu.CompilerParams(
            dimension_semantics=("parallel",), vmem_limit_bytes=VMEM_LIMIT),
        name="merge",
    )(hm, hd, hr, proj, proj, proj, x2, wm, wd, wr, wo, gain)


def _ffn_kernel(x_ref, g2_ref, g3_ref, wa_ref, wu_ref, wo_ref, out_ref, *, hc):
    x = x_ref[...]
    h = (_rms(x) * g2_ref[...]).astype(BF16)
    f = None
    for c in range(FFN_HIDDEN // hc):
        sl = slice(c * hc, (c + 1) * hc)
        a = _dot(h, wa_ref[:, sl])
        u = _dot(h, wu_ref[:, sl])
        t = ((a * _sigmoid(a)) * u).astype(BF16)
        part = _dot(t, wo_ref[sl, :])
        f = part if f is None else f + part
    out_ref[...] = x + _rms(f) * g3_ref[...]


def _ffn(x2, g2, g3, wa, wu, wo, tm, hc):
    T, D = x2.shape
    Hd = wa.shape[1]
    vec = pl.BlockSpec((1, D), lambda i: (0, 0))
    return pl.pallas_call(
        functools.partial(_ffn_kernel, hc=hc),
        grid=(T // tm,),
        in_specs=[pl.BlockSpec((tm, D), lambda i: (i, 0)), vec, vec,
                  pl.BlockSpec((D, Hd), lambda i: (0, 0)),
                  pl.BlockSpec((D, Hd), lambda i: (0, 0)),
                  pl.BlockSpec((Hd, D), lambda i: (0, 0))],
        out_specs=pl.BlockSpec((tm, D), lambda i: (i, 0)),
        out_shape=jax.ShapeDtypeStruct((T, D), F32),
        compiler_params=pltpu.CompilerParams(
            dimension_semantics=("parallel",), vmem_limit_bytes=VMEM_LIMIT),
        name="ffn",
    )(x2, g2, g3, wa, wu, wo)


def _tiles(B, S):
    T = B * S
    return dict(
        tm_in=min(2048, T), tn_in=1024,
        lc=min(256, S), bq=min(512, S),
        tm_merge=min(512, T), tm_ffn=min(512, T), hc=1408,
    )


def kernel(x, norm_gains, w_in, mlstm_conv, mlstm_gate_bias, diff_lambda, diff_subln,
           w_mlstm_out, w_diff_out, w_ret_out, w_out, w_ffn_in, w_ffn_out):
    B, S, D = x.shape
    depth = w_in.shape[0]
    T = B * S
    cfg = _tiles(B, S)
    x2 = x.reshape(T, D).astype(F32)
    W = MLSTM_HEADS * MLSTM_HD

    for l in range(depth):
        g = norm_gains[l].astype(F32)
        wl = w_in[l]
        w_main = jnp.concatenate([wl[:, :4 * W], wl[:, 4 * W + 8:]], axis=1).astype(BF16)
        w_if = jnp.pad(wl[:, 4 * W:4 * W + 8], ((0, 0), (0, GATE_PAD - 8))).astype(BF16)
        proj, gates = _inproj(x2, g[0:1], w_main, w_if, cfg["tm_in"], cfg["tn_in"])

        gate_bias = jnp.pad(mlstm_gate_bias[l].astype(F32).reshape(1, 8), ((0, 0), (0, GATE_PAD - 8)))
        hm = _mlstm(proj, gates, gate_bias, mlstm_conv[l].astype(F32), B, S, cfg["lc"])

        lam_init = 0.8 - 0.6 * math.exp(-0.3 * l)
        hd = _dattn(proj, diff_lambda[l].astype(F32), diff_subln[l].astype(F32).reshape(1, 128),
                    B, S, cfg["bq"], lam_init)

        hr = _retention(proj, B, S, cfg["lc"])

        x2 = _merge(hm, hd, hr, proj, x2,
                    w_mlstm_out[l].astype(BF16), w_diff_out[l].astype(BF16),
                    w_ret_out[l].astype(BF16), w_out[l].astype(BF16), g[1:2], cfg["tm_merge"])

        wf = w_ffn_in[l]
        x2 = _ffn(x2, g[2:3], g[3:4], wf[:, :FFN_HIDDEN].astype(BF16), wf[:, FFN_HIDDEN:].astype(BF16),
                  w_ffn_out[l].astype(BF16), cfg["tm_ffn"], cfg["hc"])

    return x2.reshape(B, S, D).astype(x.dtype)
```

```python
import functools
import math

import numpy as np
import jax
import jax.numpy as jnp
from jax import lax
from jax.experimental import pallas as pl
from jax.experimental.pallas import tpu as pltpu

F32 = jnp.float32
BF16 = jnp.bfloat16

EPS = 1e-6
CHUNK = 64
D_MODEL = 1024
MLSTM_HEADS = 4
MLSTM_HD = 256
DIFF_HEADS = 8
DIFF_HD = 64
RET_HEADS = 4
RET_HD = 256
CONV_K = 4
FFN_HIDDEN = 2816
N_MAIN = 14 * D_MODEL
GATE_PAD = 128
NEG_BIG = -1e30
LOG2E = math.log2(math.e)


def _bf16_pieces(x, n=3):
    out, r = [], np.float64(x)
    for _ in range(n):
        piece = float(np.asarray(r, dtype=jnp.bfloat16).astype(np.float64))
        out.append(piece)
        r = r - piece
    return tuple(out)


LOG2E_PIECES = _bf16_pieces(LOG2E)

VMEM_LIMIT = 56 * 1024 * 1024


def _dot(a, b):
    return jnp.dot(a, b, preferred_element_type=F32)


def _dot_nt(a, b):
    return lax.dot_general(a, b, (((1,), (1,)), ((), ())), preferred_element_type=F32)


def _dot_tn(a, b):
    return lax.dot_general(a, b, (((0,), (0,)), ((), ())), preferred_element_type=F32)


def _sigmoid(x):
    return 1.0 / (1.0 + jnp.exp(-x))


def _rms(x, axis=-1):
    return x * lax.rsqrt(jnp.mean(x * x, axis=axis, keepdims=True) + EPS)


def _inproj_kernel(x_ref, g_ref, w_ref, wif_ref, o_ref, oif_ref, h_ref):
    @pl.when(pl.program_id(1) == 0)
    def _():
        x = x_ref[...]
        h = (_rms(x) * g_ref[...]).astype(BF16)
        h_ref[...] = h
        oif_ref[...] = _dot(h, wif_ref[...])

    o_ref[...] = _dot(h_ref[...], w_ref[...]).astype(BF16)


def _inproj(x2, gain, w_main, w_if, tm, tn):
    T, D = x2.shape
    n = w_main.shape[1]
    return pl.pallas_call(
        _inproj_kernel,
        grid=(T // tm, n // tn),
        in_specs=[
            pl.BlockSpec((tm, D), lambda i, j: (i, 0)),
            pl.BlockSpec((1, D), lambda i, j: (0, 0)),
            pl.BlockSpec((D, tn), lambda i, j: (0, j)),
            pl.BlockSpec((D, GATE_PAD), lambda i, j: (0, 0)),
        ],
        out_specs=[
            pl.BlockSpec((tm, tn), lambda i, j: (i, j)),
            pl.BlockSpec((tm, GATE_PAD), lambda i, j: (i, 0)),
        ],
        out_shape=[
            jax.ShapeDtypeStruct((T, n), BF16),
            jax.ShapeDtypeStruct((T, GATE_PAD), F32),
        ],
        scratch_shapes=[pltpu.VMEM((tm, D), BF16)],
        compiler_params=pltpu.CompilerParams(
            dimension_semantics=("parallel", "arbitrary"), vmem_limit_bytes=VMEM_LIMIT),
        name="inproj",
    )(x2, gain, w_main, w_if)


def _split3(x):
    a = x.astype(BF16)
    r = x - a.astype(F32)
    b = r.astype(BF16)
    c = (r - b.astype(F32)).astype(BF16)
    return a, b, c


def _mlstm_kernel(q_ref, k_ref, v_ref, o_ref, gt_ref, gb_ref, cw_ref, out_ref,
                  xq_ref, xk_ref, c_ref, n_ref, m_ref, *, lc):
    H, dh = MLSTM_HEADS, MLSTM_HD
    W = H * dh

    @pl.when(pl.program_id(1) == 0)
    def _():
        xq_ref[0:8, :] = jnp.zeros((8, W), F32)
        xk_ref[0:8, :] = jnp.zeros((8, W), F32)
        c_ref[...] = jnp.zeros_like(c_ref)
        n_ref[...] = jnp.zeros_like(n_ref)
        m_ref[...] = jnp.zeros_like(m_ref)

    def conv_silu(x_ref, st_ref, w):
        st_ref[8:8 + lc, :] = x_ref[...].astype(F32)
        z = st_ref[5:5 + lc, :] * w[0:1, :]
        z = z + st_ref[6:6 + lc, :] * w[1:2, :]
        z = z + st_ref[7:7 + lc, :] * w[2:3, :]
        z = z + st_ref[8:8 + lc, :] * w[3:4, :]
        tail = st_ref[lc:lc + 8, :]
        st_ref[0:8, :] = tail
        return z * _sigmoid(z)

    cw = cw_ref[...]
    qc = conv_silu(q_ref, xq_ref, cw[:, 0:W])
    kc = conv_silu(k_ref, xk_ref, cw[:, W:2 * W]) * (dh ** -0.5)

    g = gt_ref[...] + gb_ref[...]
    logf = jnp.minimum(g, 0.0) - jnp.log(1.0 + jnp.exp(-jnp.abs(g)))
    row = lax.broadcasted_iota(jnp.int32, (lc, lc), 0)
    col = lax.broadcasted_iota(jnp.int32, (lc, lc), 1)
    causal = row >= col
    tril = causal.astype(BF16)
    f1, f2, f3 = _split3(logf)
    bc = _dot(tril, f1) + _dot(tril, f2) + _dot(tril, f3)
    bt = bc.T
    gtr = g.T

    for h in range(H):
        sl = slice(h * dh, (h + 1) * dh)
        qh = qc[:, sl]
        kh = kc[:, sl]
        qb = qh.astype(BF16)
        kb = kh.astype(BF16)
        vb = v_ref[:, sl]
        b_col = bc[:, 4 + h:5 + h]
        i_col = g[:, h:h + 1]
        b_row = bt[4 + h:5 + h, :]
        i_row = gtr[h:h + 1, :]
        gsum = bc[lc - 1:lc, 4 + h:5 + h]
        m_old = m_ref[h:h + 1, 0:1]
        c_old = c_ref[h]
        n_old = n_ref[h:h + 1, :]

        d = jnp.where(causal, b_col - b_row + i_row, NEG_BIG)
        d_inter = b_col + m_old
        m_row = jnp.maximum(jnp.max(d, axis=-1, keepdims=True), d_inter)
        w_intra = jnp.exp(d - m_row)
        w_inter = jnp.exp(d_inter - m_row)
        s = _dot_nt(qb, kb) * w_intra
        num = _dot(s.astype(BF16), vb) + _dot(qb, c_old.astype(BF16)) * w_inter
        den = (jnp.sum(s, axis=-1, keepdims=True)
               + jnp.sum(qh * n_old, axis=-1, keepdims=True) * w_inter)
        den = jnp.maximum(jnp.abs(den), jnp.exp(-m_row))
        hh = num / den
        og = o_ref[:, sl].astype(F32)
        out_ref[:, sl] = (_sigmoid(og) * _rms(hh)).astype(BF16)

        a_col = gsum - b_col + i_col
        m_new = jnp.maximum(gsum + m_old, jnp.max(a_col, axis=0, keepdims=True))
        w_old = jnp.exp(gsum + m_old - m_new)
        kw = kh * jnp.exp(a_col - m_new)
        c_ref[h] = w_old * c_old + _dot_tn(kw.astype(BF16), vb)
        n_ref[h:h + 1, :] = w_old * n_old + jnp.sum(kw, axis=0, keepdims=True)
        m_ref[h:h + 1, :] = jnp.broadcast_to(m_new, (1, 128))


def _mlstm(proj, gates, gate_bias, conv_w, B, S, lc):
    T = B * S
    nc = S // lc
    W = MLSTM_HEADS * MLSTM_HD
    blk = lambda c0: pl.BlockSpec((lc, W), lambda b, c: (b * nc + c, c0))
    return pl.pallas_call(
        functools.partial(_mlstm_kernel, lc=lc),
        grid=(B, nc),
        in_specs=[
            blk(0), blk(1), blk(2), blk(3),
            pl.BlockSpec((lc, GATE_PAD), lambda b, c: (b * nc + c, 0)),
            pl.BlockSpec((1, GATE_PAD), lambda b, c: (0, 0)),
            pl.BlockSpec((CONV_K, 2 * W), lambda b, c: (0, 0)),
        ],
        out_specs=pl.BlockSpec((lc, W), lambda b, c: (b * nc + c, 0)),
        out_shape=jax.ShapeDtypeStruct((T, W), BF16),
        scratch_shapes=[
            pltpu.VMEM((lc + 8, W), F32),
            pltpu.VMEM((lc + 8, W), F32),
            pltpu.VMEM((MLSTM_HEADS, MLSTM_HD, MLSTM_HD), F32),
            pltpu.VMEM((8, MLSTM_HD), F32),
            pltpu.VMEM((8, 128), F32),
        ],
        compiler_params=pltpu.CompilerParams(
            dimension_semantics=("arbitrary", "arbitrary"), vmem_limit_bytes=VMEM_LIMIT),
        name="mlstm",
    )(proj, proj, proj, proj, gates, gate_bias, conv_w)


def _ret_kernel(q_ref, k_ref, v_ref, g_ref, out_ref, r_ref, dm_ref, *, lc):
    H, dh = RET_HEADS, RET_HD

    @pl.when(pl.program_id(1) == 0)
    def _():
        r_ref[...] = jnp.zeros_like(r_ref)

    @pl.when((pl.program_id(0) == 0) & (pl.program_id(1) == 0))
    def _():
        row = lax.broadcasted_iota(jnp.int32, (lc, lc), 0)
        col = lax.broadcasted_iota(jnp.int32, (lc, lc), 1)
        rel = (row - col).astype(F32)
        for h in range(H):
            lg = math.log(1.0 - 2.0 ** (-5.0 - h))
            dm_ref[h] = jnp.where(rel >= 0.0, jnp.exp(lg * jnp.maximum(rel, 0.0)), 0.0)

    pos = lax.broadcasted_iota(jnp.int32, (lc, 1), 0).astype(F32)
    for h in range(H):
        lg = math.log(1.0 - 2.0 ** (-5.0 - h))
        sl = slice(h * dh, (h + 1) * dh)
        qb = q_ref[:, sl]
        kf = k_ref[:, sl].astype(F32) * (dh ** -0.5)
        vb = v_ref[:, sl]
        r_old = r_ref[h]
        xi = jnp.exp(lg * (pos + 1.0))
        zeta = jnp.exp(lg * (lc - 1.0 - pos))
        s = _dot_nt(qb, kf.astype(BF16)) * dm_ref[h]
        o = _dot(s.astype(BF16), vb) + _dot(qb, r_old.astype(BF16)) * xi
        r_ref[h] = math.exp(lg * lc) * r_old + _dot_tn((kf * zeta).astype(BF16), vb)
        gg = g_ref[:, sl].astype(F32)
        out_ref[:, sl] = (gg * _sigmoid(gg) * _rms(o)).astype(BF16)


def _retention(proj, B, S, lc):
    T = B * S
    nc = S // lc
    W = RET_HEADS * RET_HD
    blk = lambda c0: pl.BlockSpec((lc, W), lambda b, c: (b * nc + c, c0))
    return pl.pallas_call(
        functools.partial(_ret_kernel, lc=lc),
        grid=(B, nc),
        in_specs=[blk(7), blk(8), blk(9), blk(10)],
        out_specs=pl.BlockSpec((lc, W), lambda b, c: (b * nc + c, 0)),
        out_shape=jax.ShapeDtypeStruct((T, W), BF16),
        scratch_shapes=[
            pltpu.VMEM((RET_HEADS, RET_HD, RET_HD), F32),
            pltpu.VMEM((RET_HEADS, lc, lc), F32),
        ],
        compiler_params=pltpu.CompilerParams(
            dimension_semantics=("arbitrary", "arbitrary"), vmem_limit_bytes=VMEM_LIMIT),
        name="retention",
    )(proj, proj, proj, proj)


def _dattn_kernel(q_ref, k_ref, v_ref, lam_ref, sub_ref, out_ref,
                  k1_ref, k2_ref, vp_ref, cd_ref, acc_ref, m_ref, qa_ref, s0_ref, s1_ref, mc0_ref, mc1_ref,
                  *, bq, lam_init):
    s_refs = (s0_ref, s1_ref)
    mc_refs = (mc0_ref, mc1_ref)
    dh = DIFF_HD
    S = k_ref.shape[0]
    h = pl.program_id(1)
    qi = pl.program_id(2)
    slope = lax.bitcast_convert_type(jnp.full((1, 1), (126 - h) << 23, jnp.int32), F32)

    @pl.when(qi == 0)
    def _():
        lane = lax.broadcasted_iota(jnp.int32, (bq, 128), 1)
        rloc = lax.broadcasted_iota(jnp.int32, (bq, 128), 0)

        def build(i, carry):
            r0 = pl.multiple_of(i * bq, bq)
            t = rloc + r0
            fa = slope * (t - (t & (CHUNK - 1))).astype(F32)
            fb = slope * (t & (CHUNK - 1)).astype(F32)
            feat = jnp.where(lane < dh + 3, fa, jnp.where(lane < dh + 6, fb,
                                                          jnp.where(lane < dh + 9, 1.0, 0.0)))
            kf = k_ref[pl.ds(r0, bq), :].astype(F32)
            k1_ref[pl.ds(r0, bq), :] = jnp.where(lane < dh, kf, feat).astype(BF16)
            k2_ref[pl.ds(r0, bq), :] = jnp.where(lane < dh, pltpu.roll(kf, dh, 1), feat).astype(BF16)
            vp_ref[pl.ds(r0, bq), 0:128] = v_ref[pl.ds(r0, bq), :]
            vp_ref[pl.ds(r0, bq), 128:256] = jnp.ones((bq, 128), BF16)
            return carry

        lax.fori_loop(0, S // bq, build, 0)
        i2 = lax.broadcasted_iota(jnp.int32, (bq, bq), 0)
        j2 = lax.broadcasted_iota(jnp.int32, (bq, bq), 1)
        fut = jnp.where(j2 > i2, (-2.0 * LOG2E * slope) * (j2 - i2).astype(F32), 0.0)
        cd_ref[...] = jnp.where((j2 >> 6) <= (i2 >> 6), fut, NEG_BIG)

    lane = lax.broadcasted_iota(jnp.int32, (bq, 128), 1)
    c = (-LOG2E * (qi * bq).astype(F32)) * slope
    c1 = c.astype(BF16).astype(F32)
    c2 = (c - c1).astype(BF16).astype(F32)
    c3 = ((c - c1) - c2).astype(BF16).astype(F32)
    l1, l2, l3 = LOG2E_PIECES
    qfeat = jnp.where((lane == dh) | (lane == dh + 3), l1,
                      jnp.where((lane == dh + 1) | (lane == dh + 4), l2,
                                jnp.where((lane == dh + 2) | (lane == dh + 5), l3,
                                          jnp.where(lane == dh + 6, c1,
                                                    jnp.where(lane == dh + 7, c2,
                                                              jnp.where(lane == dh + 8, c3, 0.0))))))
    qf = q_ref[...].astype(F32) * (LOG2E * dh ** -0.5)
    qa_ref[0] = jnp.where(lane < dh, qf, qfeat).astype(BF16)
    qa_ref[1] = jnp.where(lane < dh, pltpu.roll(qf, dh, 1), qfeat).astype(BF16)
    kas = (k1_ref, k2_ref)

    m_ref[...] = jnp.full(m_ref.shape, NEG_BIG, F32)
    acc_ref[...] = jnp.zeros(acc_ref.shape, F32)

    def scores(blk, diag, slot):
        k0 = pl.multiple_of(blk * bq, bq)
        for c_ in range(2):
            sc = _dot_nt(qa_ref[c_], kas[c_][pl.ds(k0, bq), :])
            if diag:
                sc = sc + cd_ref[...]
            s_refs[slot][c_] = sc
            mc_refs[slot][c_] = jnp.broadcast_to(jnp.max(sc, axis=-1, keepdims=True), (bq, 128))

    def update(blk, slot):
        vb = vp_ref[pl.ds(pl.multiple_of(blk * bq, bq), bq), :]
        for c_ in range(2):
            m_prev = m_ref[c_]
            m_next = jnp.maximum(m_prev, mc_refs[slot][c_])
            p = jnp.exp2(s_refs[slot][c_] - jnp.concatenate([m_next] * (bq // 128), axis=1))
            alpha = jnp.exp2(m_prev - m_next)
            acc_ref[c_] = (jnp.concatenate([alpha, alpha], axis=1) * acc_ref[c_]
                           + _dot(p.astype(BF16), vb))
            m_ref[c_] = m_next

    last = jnp.maximum(qi - 1, 0)
    scores(qi, True, 0)

    def body(pr, carry):
        t = 2 * pr
        scores(jnp.minimum(t, last), False, 1)
        update(jnp.where(t == 0, qi, t - 1), 0)

        @pl.when(t + 1 <= qi)
        def _():
            scores(jnp.minimum(t + 1, last), False, 0)
            update(t, 1)

        return carry

    lax.fori_loop(0, (qi + 2) // 2, body, 0)

    lm = lam_ref[...]
    lam = (jnp.exp(jnp.sum(lm[0:1, :] * lm[1:2, :], axis=-1, keepdims=True))
           - jnp.exp(jnp.sum(lm[2:3, :] * lm[3:4, :], axis=-1, keepdims=True)) + lam_init)
    a0 = acc_ref[0]
    a1 = acc_ref[1]
    o = a0[:, 0:128] / a0[:, 128:256] - lam * (a1[:, 0:128] / a1[:, 128:256])
    out_ref[...] = ((_rms(o) * sub_ref[...]) * (1.0 - lam_init)).astype(BF16)


def _dattn(proj, lam_p, subln, B, S, bq, lam_init):
    T = B * S
    nq = S // bq
    W = DIFF_HEADS * 2 * DIFF_HD
    return pl.pallas_call(
        functools.partial(_dattn_kernel, bq=bq, lam_init=lam_init),
        grid=(B, DIFF_HEADS, nq),
        in_specs=[
            pl.BlockSpec((bq, 128), lambda b, h, i: (b * nq + i, 32 + h)),
            pl.BlockSpec((S, 128), lambda b, h, i: (b, 40 + h)),
            pl.BlockSpec((S, 128), lambda b, h, i: (b, 48 + h)),
            pl.BlockSpec((4, DIFF_HD), lambda b, h, i: (0, 0)),
            pl.BlockSpec((1, 128), lambda b, h, i: (0, 0)),
        ],
        out_specs=pl.BlockSpec((bq, 128), lambda b, h, i: (b * nq + i, h)),
        out_shape=jax.ShapeDtypeStruct((T, W), BF16),
        scratch_shapes=[
            pltpu.VMEM((S, 128), BF16),
            pltpu.VMEM((S, 128), BF16),
            pltpu.VMEM((S, 256), BF16),
            pltpu.VMEM((bq, bq), F32),
            pltpu.VMEM((2, bq, 256), F32),
            pltpu.VMEM((2, bq, 128), F32),
            pltpu.VMEM((2, bq, 128), BF16),
            pltpu.VMEM((2, bq, bq), F32),
            pltpu.VMEM((2, bq, bq), F32),
            pltpu.VMEM((2, bq, 128), F32),
            pltpu.VMEM((2, bq, 128), F32),
        ],
        compiler_params=pltpu.CompilerParams(
            dimension_semantics=("arbitrary", "arbitrary", "arbitrary"),
            vmem_limit_bytes=VMEM_LIMIT),
        name="dattn",
    )(proj, proj, proj, lam_p, subln)


def _merge_kernel(hm_ref, hd_ref, hr_ref, gm_ref, gd_ref, gr_ref, x_ref,
                  wm_ref, wd_ref, wr_ref, wo_ref, g_ref, out_ref):
    y = _sigmoid(gm_ref[...].astype(F32)) * _dot(hm_ref[...], wm_ref[...])
    y = y + _sigmoid(gd_ref[...].astype(F32)) * _dot(hd_ref[...], wd_ref[...])
    y = y + _sigmoid(gr_ref[...].astype(F32)) * _dot(hr_ref[...], wr_ref[...])
    y2 = _dot(y.astype(BF16), wo_ref[...])
    out_ref[...] = x_ref[...] + _rms(y2) * g_ref[...]


def _merge(hm, hd, hr, proj, x2, wm, wd, wr, wo, gain, tm):
    T, D = x2.shape
    row = lambda c0: pl.BlockSpec((tm, D), lambda i: (i, c0))
    wspec = pl.BlockSpec((D, D), lambda i: (0, 0))
    return pl.pallas_call(
        _merge_kernel,
        grid=(T // tm,),
        in_specs=[row(0), row(0), row(0), row(11), row(12), row(13), row(0),
                  wspec, wspec, wspec, wspec, pl.BlockSpec((1, D), lambda i: (0, 0))],
        out_specs=row(0),
        out_shape=jax.ShapeDtypeStruct((T, D), F32),
        compiler_params=pltpu.CompilerParams(
            dimension_semantics=("parallel",), vmem_limit_bytes=VMEM_LIMIT),
        name="merge",
    )(hm, hd, hr, proj, proj, proj, x2, wm, wd, wr, wo, gain)


def _ffn_kernel(x_ref, g2_ref, g3_ref, wa_ref, wu_ref, wo_ref, out_ref, *, hc):
    x = x_ref[...]
    h = (_rms(x) * g2_ref[...]).astype(BF16)
    f = None
    for c in range(FFN_HIDDEN // hc):
        sl = slice(c * hc, (c + 1) * hc)
        a = _dot(h, wa_ref[:, sl])
        u = _dot(h, wu_ref[:, sl])
        t = ((a * _sigmoid(a)) * u).astype(BF16)
        part = _dot(t, wo_ref[sl, :])
        f = part if f is None else f + part
    out_ref[...] = x + _rms(f) * g3_ref[...]


def _ffn(x2, g2, g3, w_in, wo, tm, hc):
    T, D = x2.shape
    Hd = wo.shape[0]
    vec = pl.BlockSpec((1, D), lambda i: (0, 0))
    return pl.pallas_call(
        functools.partial(_ffn_kernel, hc=hc),
        grid=(T // tm,),
        in_specs=[pl.BlockSpec((tm, D), lambda i: (i, 0)), vec, vec,
                  pl.BlockSpec((D, Hd), lambda i: (0, 0)),
                  pl.BlockSpec((D, Hd), lambda i: (0, 1)),
                  pl.BlockSpec((Hd, D), lambda i: (0, 0))],
        out_specs=pl.BlockSpec((tm, D), lambda i: (i, 0)),
        out_shape=jax.ShapeDtypeStruct((T, D), F32),
        compiler_params=pltpu.CompilerParams(
            dimension_semantics=("parallel",), vmem_limit_bytes=VMEM_LIMIT),
        name="ffn",
    )(x2, g2, g3, w_in, w_in, wo)


def _tiles(B, S):
    T = B * S
    return dict(
        tm_in=min(2048, T), tn_in=1024,
        lc=min(256, S), bq=min(512, S),
        tm_merge=min(512, T), tm_ffn=min(512, T), hc=1408,
    )


def kernel(x, norm_gains, w_in, mlstm_conv, mlstm_gate_bias, diff_lambda, diff_subln,
           w_mlstm_out, w_diff_out, w_ret_out, w_out, w_ffn_in, w_ffn_out):
    B, S, D = x.shape
    depth = w_in.shape[0]
    T = B * S
    cfg = _tiles(B, S)
    x2 = x.reshape(T, D).astype(F32)
    W = MLSTM_HEADS * MLSTM_HD

    for l in range(depth):
        g = norm_gains[l].astype(F32)
        wl = w_in[l]
        w_main = jnp.concatenate([wl[:, :4 * W], wl[:, 4 * W + 8:]], axis=1).astype(BF16)
        w_if = jnp.pad(wl[:, 4 * W:4 * W + 8], ((0, 0), (0, GATE_PAD - 8))).astype(BF16)
        proj, gates = _inproj(x2, g[0:1], w_main, w_if, cfg["tm_in"], cfg["tn_in"])

        gate_bias = jnp.pad(mlstm_gate_bias[l].astype(F32).reshape(1, 8), ((0, 0), (0, GATE_PAD - 8)))
        hm = _mlstm(proj, gates, gate_bias, mlstm_conv[l].astype(F32), B, S, cfg["lc"])

        lam_init = 0.8 - 0.6 * math.exp(-0.3 * l)
        hd = _dattn(proj, diff_lambda[l].astype(F32), diff_subln[l].astype(F32).reshape(1, 128),
                    B, S, cfg["bq"], lam_init)

        hr = _retention(proj, B, S, cfg["lc"])

        x2 = _merge(hm, hd, hr, proj, x2,
                    w_mlstm_out[l].astype(BF16), w_diff_out[l].astype(BF16),
                    w_ret_out[l].astype(BF16), w_out[l].astype(BF16), g[1:2], cfg["tm_merge"])

        x2 = _ffn(x2, g[2:3], g[3:4], w_ffn_in[l].astype(BF16), w_ffn_out[l].astype(BF16),
                  cfg["tm_ffn"], cfg["hc"])

    return x2.reshape(B, S, D).astype(x.dtype)
```

```python
import functools
import math

import numpy as np
import jax
import jax.numpy as jnp
from jax import lax
from jax.experimental import pallas as pl
from jax.experimental.pallas import tpu as pltpu

F32 = jnp.float32
BF16 = jnp.bfloat16

EPS = 1e-6
CHUNK = 64
D_MODEL = 1024
MLSTM_HEADS = 4
MLSTM_HD = 256
DIFF_HEADS = 8
DIFF_HD = 64
RET_HEADS = 4
RET_HD = 256
CONV_K = 4
FFN_HIDDEN = 2816
N_MAIN = 14 * D_MODEL
GATE_PAD = 128
NEG_BIG = -1e30
LOG2E = math.log2(math.e)


def _bf16_pieces(x, n=3):
    out, r = [], np.float64(x)
    for _ in range(n):
        piece = float(np.asarray(r, dtype=jnp.bfloat16).astype(np.float64))
        out.append(piece)
        r = r - piece
    return tuple(out)


LOG2E_PIECES = _bf16_pieces(LOG2E)

VMEM_LIMIT = 56 * 1024 * 1024


def _dot(a, b):
    return jnp.dot(a, b, preferred_element_type=F32)


def _dot_nt(a, b):
    return lax.dot_general(a, b, (((1,), (1,)), ((), ())), preferred_element_type=F32)


def _dot_tn(a, b):
    return lax.dot_general(a, b, (((0,), (0,)), ((), ())), preferred_element_type=F32)


def _sigmoid(x):
    return 1.0 / (1.0 + jnp.exp(-x))


def _rms(x, axis=-1):
    return x * lax.rsqrt(jnp.mean(x * x, axis=axis, keepdims=True) + EPS)


def _inproj_kernel(x_ref, g_ref, w_ref, wif_ref, o_ref, oif_ref, h_ref):
    @pl.when(pl.program_id(1) == 0)
    def _():
        x = x_ref[...]
        h = (_rms(x) * g_ref[...]).astype(BF16)
        h_ref[...] = h
        oif_ref[...] = _dot(h, wif_ref[...])

    o_ref[...] = _dot(h_ref[...], w_ref[...]).astype(BF16)


def _inproj(x2, gains, w_main, w_if, l, tm, tn):
    T, D = x2.shape
    n = w_main.shape[2]
    return pl.pallas_call(
        _inproj_kernel,
        grid=(T // tm, n // tn),
        in_specs=[
            pl.BlockSpec((tm, D), lambda i, j: (i, 0)),
            pl.BlockSpec((None, None, 1, D), lambda i, j: (l, 0, 0, 0)),
            pl.BlockSpec((None, D, tn), lambda i, j: (l, 0, j)),
            pl.BlockSpec((None, D, GATE_PAD), lambda i, j: (l, 0, 0)),
        ],
        out_specs=[
            pl.BlockSpec((tm, tn), lambda i, j: (i, j)),
            pl.BlockSpec((tm, GATE_PAD), lambda i, j: (i, 0)),
        ],
        out_shape=[
            jax.ShapeDtypeStruct((T, n), BF16),
            jax.ShapeDtypeStruct((T, GATE_PAD), F32),
        ],
        scratch_shapes=[pltpu.VMEM((tm, D), BF16)],
        compiler_params=pltpu.CompilerParams(
            dimension_semantics=("parallel", "arbitrary"), vmem_limit_bytes=VMEM_LIMIT),
        name="inproj",
    )(x2, gains, w_main, w_if)


def _split3(x):
    a = x.astype(BF16)
    r = x - a.astype(F32)
    b = r.astype(BF16)
    c = (r - b.astype(F32)).astype(BF16)
    return a, b, c


def _mlstm_kernel(q_ref, k_ref, v_ref, o_ref, gt_ref, gb_ref, cw_ref, out_ref,
                  xq_ref, xk_ref, c_ref, n_ref, m_ref, *, lc):
    H, dh = MLSTM_HEADS, MLSTM_HD
    W = H * dh

    @pl.when(pl.program_id(1) == 0)
    def _():
        xq_ref[...] = jnp.zeros_like(xq_ref)
        xk_ref[...] = jnp.zeros_like(xk_ref)
        c_ref[...] = jnp.zeros_like(c_ref)
        n_ref[...] = jnp.zeros_like(n_ref)
        m_ref[...] = jnp.zeros_like(m_ref)

    row = lax.broadcasted_iota(jnp.int32, (lc, lc), 0)
    col = lax.broadcasted_iota(jnp.int32, (lc, lc), 1)
    causal = row >= col
    rid = lax.broadcasted_iota(jnp.int32, (8, W), 0)
    shifts = [jnp.where(row - col == j, 1.0, 0.0).astype(BF16) for j in range(1, CONV_K)]

    def conv_silu(x_ref, tail_ref, w):
        xb = x_ref[...]
        z = xb.astype(F32) * w[3:4, :]
        for j in range(1, CONV_K):
            z = z + _dot(shifts[j - 1], xb) * w[3 - j:4 - j, :]
        tl = tail_ref[...]
        p1, p2, p3 = tl[7:8, :], tl[6:7, :], tl[5:6, :]
        head = jnp.where(rid == 0, p1 * w[2:3, :] + p2 * w[1:2, :] + p3 * w[0:1, :],
                         jnp.where(rid == 1, p1 * w[1:2, :] + p2 * w[0:1, :],
                                   jnp.where(rid == 2, p1 * w[0:1, :], 0.0)))
        z = z + jnp.concatenate([head, jnp.zeros((lc - 8, W), F32)], axis=0)
        tail_ref[...] = xb[lc - 8:lc, :].astype(F32)
        return z * _sigmoid(z)

    cw = cw_ref[...]
    qc = conv_silu(q_ref, xq_ref, cw[:, 0:W])
    kc = conv_silu(k_ref, xk_ref, cw[:, W:2 * W]) * (dh ** -0.5)

    g = gt_ref[...] + gb_ref[...]
    logf = jnp.minimum(g, 0.0) - jnp.log(1.0 + jnp.exp(-jnp.abs(g)))
    tril = jnp.where(causal, 1.0, 0.0).astype(BF16)
    f1, f2, f3 = _split3(logf)
    bc = _dot(tril, f1) + _dot(tril, f2) + _dot(tril, f3)
    bt = bc.T
    gtr = g.T

    for h in range(H):
        sl = slice(h * dh, (h + 1) * dh)
        qh = qc[:, sl]
        kh = kc[:, sl]
        qb = qh.astype(BF16)
        kb = kh.astype(BF16)
        vb = v_ref[:, sl]
        b_col = bc[:, 4 + h:5 + h]
        i_col = g[:, h:h + 1]
        b_row = bt[4 + h:5 + h, :]
        i_row = gtr[h:h + 1, :]
        gsum = bc[lc - 1:lc, 4 + h:5 + h]
        m_old = m_ref[h:h + 1, 0:1]
        c_old = c_ref[h]
        n_old = n_ref[h:h + 1, :]

        d = jnp.where(causal, b_col - b_row + i_row, NEG_BIG)
        d_inter = b_col + m_old
        m_row = jnp.maximum(jnp.max(d, axis=-1, keepdims=True), d_inter)
        w_intra = jnp.exp(d - m_row)
        w_inter = jnp.exp(d_inter - m_row)
        s = _dot_nt(qb, kb) * w_intra
        num = _dot(s.astype(BF16), vb) + _dot(qb, c_old.astype(BF16)) * w_inter
        den = (jnp.sum(s, axis=-1, keepdims=True)
               + jnp.sum(qh * n_old, axis=-1, keepdims=True) * w_inter)
        den = jnp.maximum(jnp.abs(den), jnp.exp(-m_row))
        hh = num / den
        og = o_ref[:, sl].astype(F32)
        out_ref[:, sl] = (_sigmoid(og) * _rms(hh)).astype(BF16)

        a_col = gsum - b_col + i_col
        m_new = jnp.maximum(gsum + m_old, jnp.max(a_col, axis=0, keepdims=True))
        w_old = jnp.exp(gsum + m_old - m_new)
        kw = kh * jnp.exp(a_col - m_new)
        c_ref[h] = w_old * c_old + _dot_tn(kw.astype(BF16), vb)
        n_ref[h:h + 1, :] = w_old * n_old + jnp.sum(kw, axis=0, keepdims=True)
        m_ref[h:h + 1, :] = jnp.broadcast_to(m_new, (1, 128))


def _mlstm(proj, gates, gate_bias, conv_w, l, B, S, lc):
    T = B * S
    nc = S // lc
    W = MLSTM_HEADS * MLSTM_HD
    blk = lambda c0: pl.BlockSpec((lc, W), lambda b, c: (b * nc + c, c0))
    return pl.pallas_call(
        functools.partial(_mlstm_kernel, lc=lc),
        grid=(B, nc),
        in_specs=[
            blk(0), blk(1), blk(2), blk(3),
            pl.BlockSpec((lc, GATE_PAD), lambda b, c: (b * nc + c, 0)),
            pl.BlockSpec((None, 1, GATE_PAD), lambda b, c: (l, 0, 0)),
            pl.BlockSpec((None, CONV_K, 2 * W), lambda b, c: (l, 0, 0)),
        ],
        out_specs=pl.BlockSpec((lc, W), lambda b, c: (b * nc + c, 0)),
        out_shape=jax.ShapeDtypeStruct((T, W), BF16),
        scratch_shapes=[
            pltpu.VMEM((8, W), F32),
            pltpu.VMEM((8, W), F32),
            pltpu.VMEM((MLSTM_HEADS, MLSTM_HD, MLSTM_HD), F32),
            pltpu.VMEM((8, MLSTM_HD), F32),
            pltpu.VMEM((8, 128), F32),
        ],
        compiler_params=pltpu.CompilerParams(
            dimension_semantics=("arbitrary", "arbitrary"), vmem_limit_bytes=VMEM_LIMIT),
        name="mlstm",
    )(proj, proj, proj, proj, gates, gate_bias, conv_w)


def _ret_kernel(q_ref, k_ref, v_ref, g_ref, out_ref, r_ref, dm_ref, *, lc):
    H, dh = RET_HEADS, RET_HD

    @pl.when(pl.program_id(1) == 0)
    def _():
        r_ref[...] = jnp.zeros_like(r_ref)

    @pl.when((pl.program_id(0) == 0) & (pl.program_id(1) == 0))
    def _():
        row = lax.broadcasted_iota(jnp.int32, (lc, lc), 0)
        col = lax.broadcasted_iota(jnp.int32, (lc, lc), 1)
        rel = (row - col).astype(F32)
        for h in range(H):
            lg = math.log(1.0 - 2.0 ** (-5.0 - h))
            dm_ref[h] = jnp.where(rel >= 0.0, jnp.exp(lg * jnp.maximum(rel, 0.0)), 0.0)

    pos = lax.broadcasted_iota(jnp.int32, (lc, 1), 0).astype(F32)
    for h in range(H):
        lg = math.log(1.0 - 2.0 ** (-5.0 - h))
        sl = slice(h * dh, (h + 1) * dh)
        qb = q_ref[:, sl]
        kf = k_ref[:, sl].astype(F32) * (dh ** -0.5)
        vb = v_ref[:, sl]
        r_old = r_ref[h]
        xi = jnp.exp(lg * (pos + 1.0))
        zeta = jnp.exp(lg * (lc - 1.0 - pos))
        s = _dot_nt(qb, kf.astype(BF16)) * dm_ref[h]
        o = _dot(s.astype(BF16), vb) + _dot(qb, r_old.astype(BF16)) * xi
        r_ref[h] = math.exp(lg * lc) * r_old + _dot_tn((kf * zeta).astype(BF16), vb)
        gg = g_ref[:, sl].astype(F32)
        out_ref[:, sl] = (gg * _sigmoid(gg) * _rms(o)).astype(BF16)


def _retention(proj, B, S, lc):
    T = B * S
    nc = S // lc
    W = RET_HEADS * RET_HD
    blk = lambda c0: pl.BlockSpec((lc, W), lambda b, c: (b * nc + c, c0))
    return pl.pallas_call(
        functools.partial(_ret_kernel, lc=lc),
        grid=(B, nc),
        in_specs=[blk(7), blk(8), blk(9), blk(10)],
        out_specs=pl.BlockSpec((lc, W), lambda b, c: (b * nc + c, 0)),
        out_shape=jax.ShapeDtypeStruct((T, W), BF16),
        scratch_shapes=[
            pltpu.VMEM((RET_HEADS, RET_HD, RET_HD), F32),
            pltpu.VMEM((RET_HEADS, lc, lc), F32),
        ],
        compiler_params=pltpu.CompilerParams(
            dimension_semantics=("arbitrary", "arbitrary"), vmem_limit_bytes=VMEM_LIMIT),
        name="retention",
    )(proj, proj, proj, proj)


def _dattn_kernel(q_ref, k_ref, v_ref, lam_ref, sub_ref, out_ref,
                  k1_ref, k2_ref, vp_ref, cd_ref, acc_ref, m_ref, qa_ref, s0_ref, s1_ref, mc0_ref, mc1_ref,
                  *, bq, lam_init):
    s_refs = (s0_ref, s1_ref)
    mc_refs = (mc0_ref, mc1_ref)
    dh = DIFF_HD
    S = k_ref.shape[0]
    h = pl.program_id(1)
    qi = pl.program_id(2)
    slope = lax.bitcast_convert_type(jnp.full((1, 1), (126 - h) << 23, jnp.int32), F32)

    @pl.when(qi == 0)
    def _():
        lane = lax.broadcasted_iota(jnp.int32, (bq, 128), 1)
        rloc = lax.broadcasted_iota(jnp.int32, (bq, 128), 0)

        def build(i, carry):
            r0 = pl.multiple_of(i * bq, bq)
            t = rloc + r0
            fa = slope * (t - (t & (CHUNK - 1))).astype(F32)
            fb = slope * (t & (CHUNK - 1)).astype(F32)
            feat = jnp.where(lane < dh + 3, fa, jnp.where(lane < dh + 6, fb,
                                                          jnp.where(lane < dh + 9, 1.0, 0.0)))
            kf = k_ref[pl.ds(r0, bq), :].astype(F32)
            k1_ref[pl.ds(r0, bq), :] = jnp.where(lane < dh, kf, feat).astype(BF16)
            k2_ref[pl.ds(r0, bq), :] = jnp.where(lane < dh, pltpu.roll(kf, dh, 1), feat).astype(BF16)
            vp_ref[pl.ds(r0, bq), 0:128] = v_ref[pl.ds(r0, bq), :]
            vp_ref[pl.ds(r0, bq), 128:256] = jnp.ones((bq, 128), BF16)
            return carry

        lax.fori_loop(0, S // bq, build, 0)
        i2 = lax.broadcasted_iota(jnp.int32, (bq, bq), 0)
        j2 = lax.broadcasted_iota(jnp.int32, (bq, bq), 1)
        fut = jnp.where(j2 > i2, (-2.0 * LOG2E * slope) * (j2 - i2).astype(F32), 0.0)
        cd_ref[...] = jnp.where((j2 >> 6) <= (i2 >> 6), fut, NEG_BIG)

    lane = lax.broadcasted_iota(jnp.int32, (bq, 128), 1)
    c = (-LOG2E * (qi * bq).astype(F32)) * slope
    c1 = c.astype(BF16).astype(F32)
    c2 = (c - c1).astype(BF16).astype(F32)
    c3 = ((c - c1) - c2).astype(BF16).astype(F32)
    l1, l2, l3 = LOG2E_PIECES
    qfeat = jnp.where((lane == dh) | (lane == dh + 3), l1,
                      jnp.where((lane == dh + 1) | (lane == dh + 4), l2,
                                jnp.where((lane == dh + 2) | (lane == dh + 5), l3,
                                          jnp.where(lane == dh + 6, c1,
                                                    jnp.where(lane == dh + 7, c2,
                                                              jnp.where(lane == dh + 8, c3, 0.0))))))
    qf = q_ref[...].astype(F32) * (LOG2E * dh ** -0.5)
    qa_ref[0] = jnp.where(lane < dh, qf, qfeat).astype(BF16)
    qa_ref[1] = jnp.where(lane < dh, pltpu.roll(qf, dh, 1), qfeat).astype(BF16)
    kas = (k1_ref, k2_ref)

    m_ref[...] = jnp.full(m_ref.shape, NEG_BIG, F32)
    acc_ref[...] = jnp.zeros(acc_ref.shape, F32)

    def scores(blk, diag, slot):
        k0 = pl.multiple_of(blk * bq, bq)
        for c_ in range(2):
            sc = _dot_nt(qa_ref[c_], kas[c_][pl.ds(k0, bq), :])
            if diag:
                sc = sc + cd_ref[...]
            s_refs[slot][c_] = sc
            mc_refs[slot][c_] = jnp.broadcast_to(jnp.max(sc, axis=-1, keepdims=True), (bq, 128))

    def update(blk, slot):
        vb = vp_ref[pl.ds(pl.multiple_of(blk * bq, bq), bq), :]
        for c_ in range(2):
            m_prev = m_ref[c_]
            m_next = jnp.maximum(m_prev, mc_refs[slot][c_])
            p = jnp.exp2(s_refs[slot][c_] - jnp.concatenate([m_next] * (bq // 128), axis=1))
            alpha = jnp.exp2(m_prev - m_next)
            acc_ref[c_] = (jnp.concatenate([alpha, alpha], axis=1) * acc_ref[c_]
                           + _dot(p.astype(BF16), vb))
            m_ref[c_] = m_next

    last = jnp.maximum(qi - 1, 0)
    scores(qi, True, 0)

    def body(pr, carry):
        t = 2 * pr
        scores(t, False, 1)
        update(jnp.where(t == 0, qi, t - 1), 0)
        scores(jnp.minimum(t + 1, last), False, 0)
        update(t, 1)
        return carry

    lax.fori_loop(0, (qi + 1) // 2, body, 0)

    @pl.when((qi & 1) == 0)
    def _():
        update(last, 0)

    lm = lam_ref[...]
    lam = (jnp.exp(jnp.sum(lm[0:1, :] * lm[1:2, :], axis=-1, keepdims=True))
           - jnp.exp(jnp.sum(lm[2:3, :] * lm[3:4, :], axis=-1, keepdims=True)) + lam_init)
    a0 = acc_ref[0]
    a1 = acc_ref[1]
    o = a0[:, 0:128] / a0[:, 128:256] - lam * (a1[:, 0:128] / a1[:, 128:256])
    out_ref[...] = ((_rms(o) * sub_ref[...]) * (1.0 - lam_init)).astype(BF16)


def _dattn(proj, lam_p, subln, l, B, S, bq, lam_init):
    T = B * S
    nq = S // bq
    W = DIFF_HEADS * 2 * DIFF_HD
    return pl.pallas_call(
        functools.partial(_dattn_kernel, bq=bq, lam_init=lam_init),
        grid=(B, DIFF_HEADS, nq),
        in_specs=[
            pl.BlockSpec((bq, 128), lambda b, h, i: (b * nq + i, 32 + h)),
            pl.BlockSpec((S, 128), lambda b, h, i: (b, 40 + h)),
            pl.BlockSpec((S, 128), lambda b, h, i: (b, 48 + h)),
            pl.BlockSpec((None, 4, DIFF_HD), lambda b, h, i: (l, 0, 0)),
            pl.BlockSpec((None, 1, 128), lambda b, h, i: (l, 0, 0)),
        ],
        out_specs=pl.BlockSpec((bq, 128), lambda b, h, i: (b * nq + i, h)),
        out_shape=jax.ShapeDtypeStruct((T, W), BF16),
        scratch_shapes=[
            pltpu.VMEM((S, 128), BF16),
            pltpu.VMEM((S, 128), BF16),
            pltpu.VMEM((S, 256), BF16),
            pltpu.VMEM((bq, bq), F32),
            pltpu.VMEM((2, bq, 256), F32),
            pltpu.VMEM((2, bq, 128), F32),
            pltpu.VMEM((2, bq, 128), BF16),
            pltpu.VMEM((2, bq, bq), F32),
            pltpu.VMEM((2, bq, bq), F32),
            pltpu.VMEM((2, bq, 128), F32),
            pltpu.VMEM((2, bq, 128), F32),
        ],
        compiler_params=pltpu.CompilerParams(
            dimension_semantics=("arbitrary", "arbitrary", "arbitrary"),
            vmem_limit_bytes=VMEM_LIMIT),
        name="dattn",
    )(proj, proj, proj, lam_p, subln)


def _merge_kernel(hm_ref, hd_ref, hr_ref, gm_ref, gd_ref, gr_ref, x_ref,
                  wm_ref, wd_ref, wr_ref, wo_ref, g_ref, out_ref):
    y = _sigmoid(gm_ref[...].astype(F32)) * _dot(hm_ref[...], wm_ref[...])
    y = y + _sigmoid(gd_ref[...].astype(F32)) * _dot(hd_ref[...], wd_ref[...])
    y = y + _sigmoid(gr_ref[...].astype(F32)) * _dot(hr_ref[...], wr_ref[...])
    y2 = _dot(y.astype(BF16), wo_ref[...])
    out_ref[...] = x_ref[...] + _rms(y2) * g_ref[...]


def _merge(hm, hd, hr, proj, x2, wm, wd, wr, wo, gains, l, tm):
    T, D = x2.shape
    row = lambda c0: pl.BlockSpec((tm, D), lambda i: (i, c0))
    wspec = pl.BlockSpec((None, D, D), lambda i: (l, 0, 0))
    return pl.pallas_call(
        _merge_kernel,
        grid=(T // tm,),
        in_specs=[row(0), row(0), row(0), row(11), row(12), row(13), row(0),
                  wspec, wspec, wspec, wspec,
                  pl.BlockSpec((None, None, 1, D), lambda i: (l, 1, 0, 0))],
        out_specs=row(0),
        out_shape=jax.ShapeDtypeStruct((T, D), F32),
        compiler_params=pltpu.CompilerParams(
            dimension_semantics=("parallel",), vmem_limit_bytes=VMEM_LIMIT),
        name="merge",
    )(hm, hd, hr, proj, proj, proj, x2, wm, wd, wr, wo, gains)


def _ffn_kernel(x_ref, g2_ref, g3_ref, wa_ref, wu_ref, wo_ref, out_ref, *, hc):
    x = x_ref[...]
    h = (_rms(x) * g2_ref[...]).astype(BF16)
    f = None
    for c in range(FFN_HIDDEN // hc):
        sl = slice(c * hc, (c + 1) * hc)
        a = _dot(h, wa_ref[:, sl])
        u = _dot(h, wu_ref[:, sl])
        t = ((a * _sigmoid(a)) * u).astype(BF16)
        part = _dot(t, wo_ref[sl, :])
        f = part if f is None else f + part
    out_ref[...] = x + _rms(f) * g3_ref[...]


def _ffn(x2, gains, w_in, wo, l, tm, hc):
    T, D = x2.shape
    Hd = wo.shape[1]
    vec = lambda r: pl.BlockSpec((None, None, 1, D), lambda i: (l, r, 0, 0))
    return pl.pallas_call(
        functools.partial(_ffn_kernel, hc=hc),
        grid=(T // tm,),
        in_specs=[pl.BlockSpec((tm, D), lambda i: (i, 0)), vec(2), vec(3),
                  pl.BlockSpec((None, D, Hd), lambda i: (l, 0, 0)),
                  pl.BlockSpec((None, D, Hd), lambda i: (l, 0, 1)),
                  pl.BlockSpec((None, Hd, D), lambda i: (l, 0, 0))],
        out_specs=pl.BlockSpec((tm, D), lambda i: (i, 0)),
        out_shape=jax.ShapeDtypeStruct((T, D), F32),
        compiler_params=pltpu.CompilerParams(
            dimension_semantics=("parallel",), vmem_limit_bytes=VMEM_LIMIT),
        name="ffn",
    )(x2, gains, gains, w_in, w_in, wo)


def _tiles(B, S):
    T = B * S
    return dict(
        tm_in=min(2048, T), tn_in=1024,
        lc=min(256, S), bq=min(512, S),
        tm_merge=min(512, T), tm_ffn=min(512, T), hc=1408,
    )


def kernel(x, norm_gains, w_in, mlstm_conv, mlstm_gate_bias, diff_lambda, diff_subln,
           w_mlstm_out, w_diff_out, w_ret_out, w_out, w_ffn_in, w_ffn_out):
    B, S, D = x.shape
    depth = w_in.shape[0]
    T = B * S
    cfg = _tiles(B, S)
    x2 = x.reshape(T, D).astype(F32)
    W = MLSTM_HEADS * MLSTM_HD

    gains = norm_gains.astype(F32).reshape(depth, 4, 1, D)
    w_main = jnp.concatenate([w_in[:, :, :4 * W], w_in[:, :, 4 * W + 8:]], axis=2).astype(BF16)
    w_if = jnp.pad(w_in[:, :, 4 * W:4 * W + 8], ((0, 0), (0, 0), (0, GATE_PAD - 8))).astype(BF16)
    gate_bias = jnp.pad(mlstm_gate_bias.astype(F32).reshape(depth, 1, 8), ((0, 0), (0, 0), (0, GATE_PAD - 8)))
    conv_w = mlstm_conv.astype(F32)
    lam_p = diff_lambda.astype(F32)
    subln = diff_subln.astype(F32).reshape(depth, 1, 2 * DIFF_HD)
    wm, wd, wr, wo = (w.astype(BF16) for w in (w_mlstm_out, w_diff_out, w_ret_out, w_out))
    wf_in, wf_out = w_ffn_in.astype(BF16), w_ffn_out.astype(BF16)

    for l in range(depth):
        proj, gates = _inproj(x2, gains, w_main, w_if, l, cfg["tm_in"], cfg["tn_in"])
        hm = _mlstm(proj, gates, gate_bias, conv_w, l, B, S, cfg["lc"])
        lam_init = 0.8 - 0.6 * math.exp(-0.3 * l)
        hd = _dattn(proj, lam_p, subln, l, B, S, cfg["bq"], lam_init)
        hr = _retention(proj, B, S, cfg["lc"])
        x2 = _merge(hm, hd, hr, proj, x2, wm, wd, wr, wo, gains, l, cfg["tm_merge"])
        x2 = _ffn(x2, gains, wf_in, wf_out, l, cfg["tm_ffn"], cfg["hc"])

    return x2.reshape(B, S, D).astype(x.dtype)
```

```python
import functools
import math

import numpy as np
import jax
import jax.numpy as jnp
from jax import lax
from jax.experimental import pallas as pl
from jax.experimental.pallas import tpu as pltpu

F32 = jnp.float32
BF16 = jnp.bfloat16

EPS = 1e-6
CHUNK = 64
D_MODEL = 1024
MLSTM_HEADS = 4
MLSTM_HD = 256
DIFF_HEADS = 8
DIFF_HD = 64
RET_HEADS = 4
RET_HD = 256
CONV_K = 4
FFN_HIDDEN = 2816
N_MAIN = 14 * D_MODEL
GATE_PAD = 128
NEG_BIG = -1e30
LOG2E = math.log2(math.e)


def _bf16_pieces(x, n=3):
    out, r = [], np.float64(x)
    for _ in range(n):
        piece = float(np.asarray(r, dtype=jnp.bfloat16).astype(np.float64))
        out.append(piece)
        r = r - piece
    return tuple(out)


LOG2E_PIECES = _bf16_pieces(LOG2E)

VMEM_LIMIT = 56 * 1024 * 1024


def _dot(a, b):
    return jnp.dot(a, b, preferred_element_type=F32)


def _dot_nt(a, b):
    return lax.dot_general(a, b, (((1,), (1,)), ((), ())), preferred_element_type=F32)


def _dot_tn(a, b):
    return lax.dot_general(a, b, (((0,), (0,)), ((), ())), preferred_element_type=F32)


def _sigmoid(x):
    return 1.0 / (1.0 + jnp.exp(-x))


def _rms(x, axis=-1):
    return x * lax.rsqrt(jnp.mean(x * x, axis=axis, keepdims=True) + EPS)


def _inproj_kernel(x_ref, g_ref, w_ref, wif_ref, o_ref, oif_ref, h_ref):
    @pl.when(pl.program_id(1) == 0)
    def _():
        x = x_ref[...]
        h = (_rms(x) * g_ref[...]).astype(BF16)
        h_ref[...] = h
        oif_ref[...] = _dot(h, wif_ref[...])

    o_ref[...] = _dot(h_ref[...], w_ref[...]).astype(BF16)


def _inproj(x2, gains, w_main, w_if, l, tm, tn):
    T, D = x2.shape
    n = w_main.shape[2]
    return pl.pallas_call(
        _inproj_kernel,
        grid=(T // tm, n // tn),
        in_specs=[
            pl.BlockSpec((tm, D), lambda i, j: (i, 0)),
            pl.BlockSpec((None, None, 1, D), lambda i, j: (l, 0, 0, 0)),
            pl.BlockSpec((None, D, tn), lambda i, j: (l, 0, j)),
            pl.BlockSpec((None, D, GATE_PAD), lambda i, j: (l, 0, 0)),
        ],
        out_specs=[
            pl.BlockSpec((tm, tn), lambda i, j: (i, j)),
            pl.BlockSpec((tm, GATE_PAD), lambda i, j: (i, 0)),
        ],
        out_shape=[
            jax.ShapeDtypeStruct((T, n), BF16),
            jax.ShapeDtypeStruct((T, GATE_PAD), F32),
        ],
        scratch_shapes=[pltpu.VMEM((tm, D), BF16)],
        compiler_params=pltpu.CompilerParams(
            dimension_semantics=("parallel", "arbitrary"), vmem_limit_bytes=VMEM_LIMIT),
        name="inproj",
    )(x2, gains, w_main, w_if)


def _split3(x):
    a = x.astype(BF16)
    r = x - a.astype(F32)
    b = r.astype(BF16)
    c = (r - b.astype(F32)).astype(BF16)
    return a, b, c


def _mlstm_kernel(q_ref, k_ref, v_ref, o_ref, gt_ref, gb_ref, cw_ref, out_ref,
                  xq_ref, xk_ref, c_ref, n_ref, m_ref, *, lc):
    H, dh = MLSTM_HEADS, MLSTM_HD
    W = H * dh

    @pl.when(pl.program_id(1) == 0)
    def _():
        xq_ref[...] = jnp.zeros_like(xq_ref)
        xk_ref[...] = jnp.zeros_like(xk_ref)
        c_ref[...] = jnp.zeros_like(c_ref)
        n_ref[...] = jnp.zeros_like(n_ref)
        m_ref[...] = jnp.zeros_like(m_ref)

    row = lax.broadcasted_iota(jnp.int32, (lc, lc), 0)
    col = lax.broadcasted_iota(jnp.int32, (lc, lc), 1)
    causal = row >= col
    rid = lax.broadcasted_iota(jnp.int32, (8, W), 0)
    shifts = [jnp.where(row - col == j, 1.0, 0.0).astype(BF16) for j in range(1, CONV_K)]

    def conv_silu(x_ref, tail_ref, w):
        xb = x_ref[...]
        z = xb.astype(F32) * w[3:4, :]
        for j in range(1, CONV_K):
            z = z + _dot(shifts[j - 1], xb) * w[3 - j:4 - j, :]
        tl = tail_ref[...]
        p1, p2, p3 = tl[7:8, :], tl[6:7, :], tl[5:6, :]
        head = jnp.where(rid == 0, p1 * w[2:3, :] + p2 * w[1:2, :] + p3 * w[0:1, :],
                         jnp.where(rid == 1, p1 * w[1:2, :] + p2 * w[0:1, :],
                                   jnp.where(rid == 2, p1 * w[0:1, :], 0.0)))
        z = z + jnp.concatenate([head, jnp.zeros((lc - 8, W), F32)], axis=0)
        tail_ref[...] = xb[lc - 8:lc, :].astype(F32)
        return z * _sigmoid(z)

    cw = cw_ref[...]
    qc = conv_silu(q_ref, xq_ref, cw[:, 0:W])
    kc = conv_silu(k_ref, xk_ref, cw[:, W:2 * W]) * (dh ** -0.5)

    g = gt_ref[...] + gb_ref[...]
    logf = jnp.minimum(g, 0.0) - jnp.log(1.0 + jnp.exp(-jnp.abs(g)))
    tril = jnp.where(causal, 1.0, 0.0).astype(BF16)
    f1, f2, f3 = _split3(logf)
    bc = _dot(tril, f1) + _dot(tril, f2) + _dot(tril, f3)
    bt = bc.T
    gtr = g.T

    for h in range(H):
        sl = slice(h * dh, (h + 1) * dh)
        qh = qc[:, sl]
        kh = kc[:, sl]
        qb = qh.astype(BF16)
        kb = kh.astype(BF16)
        vb = v_ref[:, sl]
        b_col = bc[:, 4 + h:5 + h]
        i_col = g[:, h:h + 1]
        b_row = bt[4 + h:5 + h, :]
        i_row = gtr[h:h + 1, :]
        gsum = bc[lc - 1:lc, 4 + h:5 + h]
        m_old = m_ref[h:h + 1, 0:1]
        c_old = c_ref[h]
        n_old = n_ref[h:h + 1, :]

        d = jnp.where(causal, b_col - b_row + i_row, NEG_BIG)
        d_inter = b_col + m_old
        m_row = jnp.maximum(jnp.max(d, axis=-1, keepdims=True), d_inter)
        w_intra = jnp.exp(d - m_row)
        w_inter = jnp.exp(d_inter - m_row)
        s = _dot_nt(qb, kb) * w_intra
        num = _dot(s.astype(BF16), vb) + _dot(qb, c_old.astype(BF16)) * w_inter
        den = (jnp.sum(s, axis=-1, keepdims=True)
               + jnp.sum(qh * n_old, axis=-1, keepdims=True) * w_inter)
        den = jnp.maximum(jnp.abs(den), jnp.exp(-m_row))
        hh = num / den
        og = o_ref[:, sl].astype(F32)
        out_ref[:, sl] = (_sigmoid(og) * _rms(hh)).astype(BF16)

        a_col = gsum - b_col + i_col
        m_new = jnp.maximum(gsum + m_old, jnp.max(a_col, axis=0, keepdims=True))
        w_old = jnp.exp(gsum + m_old - m_new)
        kw = kh * jnp.exp(a_col - m_new)
        c_ref[h] = w_old * c_old + _dot_tn(kw.astype(BF16), vb)
        n_ref[h:h + 1, :] = w_old * n_old + jnp.sum(kw, axis=0, keepdims=True)
        m_ref[h:h + 1, :] = jnp.broadcast_to(m_new, (1, 128))


def _mlstm(proj, gates, gate_bias, conv_w, l, B, S, lc):
    T = B * S
    nc = S // lc
    W = MLSTM_HEADS * MLSTM_HD
    blk = lambda c0: pl.BlockSpec((lc, W), lambda b, c: (b * nc + c, c0))
    return pl.pallas_call(
        functools.partial(_mlstm_kernel, lc=lc),
        grid=(B, nc),
        in_specs=[
            blk(0), blk(1), blk(2), blk(3),
            pl.BlockSpec((lc, GATE_PAD), lambda b, c: (b * nc + c, 0)),
            pl.BlockSpec((None, 1, GATE_PAD), lambda b, c: (l, 0, 0)),
            pl.BlockSpec((None, CONV_K, 2 * W), lambda b, c: (l, 0, 0)),
        ],
        out_specs=pl.BlockSpec((lc, W), lambda b, c: (b * nc + c, 0)),
        out_shape=jax.ShapeDtypeStruct((T, W), BF16),
        scratch_shapes=[
            pltpu.VMEM((8, W), F32),
            pltpu.VMEM((8, W), F32),
            pltpu.VMEM((MLSTM_HEADS, MLSTM_HD, MLSTM_HD), F32),
            pltpu.VMEM((8, MLSTM_HD), F32),
            pltpu.VMEM((8, 128), F32),
        ],
        compiler_params=pltpu.CompilerParams(
            dimension_semantics=("arbitrary", "arbitrary"), vmem_limit_bytes=VMEM_LIMIT),
        name="mlstm",
    )(proj, proj, proj, proj, gates, gate_bias, conv_w)


def _ret_kernel(q_ref, k_ref, v_ref, g_ref, out_ref, r_ref, dm_ref, *, lc):
    H, dh = RET_HEADS, RET_HD

    @pl.when(pl.program_id(1) == 0)
    def _():
        r_ref[...] = jnp.zeros_like(r_ref)

    @pl.when((pl.program_id(0) == 0) & (pl.program_id(1) == 0))
    def _():
        row = lax.broadcasted_iota(jnp.int32, (lc, lc), 0)
        col = lax.broadcasted_iota(jnp.int32, (lc, lc), 1)
        rel = (row - col).astype(F32)
        for h in range(H):
            lg = math.log(1.0 - 2.0 ** (-5.0 - h))
            dm_ref[h] = jnp.where(rel >= 0.0, jnp.exp(lg * jnp.maximum(rel, 0.0)), 0.0)

    pos = lax.broadcasted_iota(jnp.int32, (lc, 1), 0).astype(F32)
    for h in range(H):
        lg = math.log(1.0 - 2.0 ** (-5.0 - h))
        sl = slice(h * dh, (h + 1) * dh)
        qb = q_ref[:, sl]
        kf = k_ref[:, sl].astype(F32) * (dh ** -0.5)
        vb = v_ref[:, sl]
        r_old = r_ref[h]
        xi = jnp.exp(lg * (pos + 1.0))
        zeta = jnp.exp(lg * (lc - 1.0 - pos))
        s = _dot_nt(qb, kf.astype(BF16)) * dm_ref[h]
        o = _dot(s.astype(BF16), vb) + _dot(qb, r_old.astype(BF16)) * xi
        r_ref[h] = math.exp(lg * lc) * r_old + _dot_tn((kf * zeta).astype(BF16), vb)
        gg = g_ref[:, sl].astype(F32)
        out_ref[:, sl] = (gg * _sigmoid(gg) * _rms(o)).astype(BF16)


def _retention(proj, B, S, lc):
    T = B * S
    nc = S // lc
    W = RET_HEADS * RET_HD
    blk = lambda c0: pl.BlockSpec((lc, W), lambda b, c: (b * nc + c, c0))
    return pl.pallas_call(
        functools.partial(_ret_kernel, lc=lc),
        grid=(B, nc),
        in_specs=[blk(7), blk(8), blk(9), blk(10)],
        out_specs=pl.BlockSpec((lc, W), lambda b, c: (b * nc + c, 0)),
        out_shape=jax.ShapeDtypeStruct((T, W), BF16),
        scratch_shapes=[
            pltpu.VMEM((RET_HEADS, RET_HD, RET_HD), F32),
            pltpu.VMEM((RET_HEADS, lc, lc), F32),
        ],
        compiler_params=pltpu.CompilerParams(
            dimension_semantics=("arbitrary", "arbitrary"), vmem_limit_bytes=VMEM_LIMIT),
        name="retention",
    )(proj, proj, proj, proj)


def _dattn_kernel(q_ref, k_ref, v_ref, lam_ref, sub_ref, out_ref,
                  k1_ref, k2_ref, vp_ref, cd_ref, acc_ref, m_ref, qa_ref, s0_ref, s1_ref, mc0_ref, mc1_ref,
                  *, bq, bk, lam_init):
    s_refs = (s0_ref, s1_ref)
    mc_refs = (mc0_ref, mc1_ref)
    dh = DIFF_HD
    S = k_ref.shape[0]
    h = pl.program_id(1)
    qi = pl.program_id(2)
    slope = lax.bitcast_convert_type(jnp.full((1, 1), (126 - h) << 23, jnp.int32), F32)

    @pl.when(qi == 0)
    def _():
        lane = lax.broadcasted_iota(jnp.int32, (bk, 128), 1)
        rloc = lax.broadcasted_iota(jnp.int32, (bk, 128), 0)

        def build(i, carry):
            r0 = pl.multiple_of(i * bk, bk)
            t = rloc + r0
            fa = slope * (t - (t & (CHUNK - 1))).astype(F32)
            fb = slope * (t & (CHUNK - 1)).astype(F32)
            feat = jnp.where(lane < dh + 3, fa, jnp.where(lane < dh + 6, fb,
                                                          jnp.where(lane < dh + 9, 1.0, 0.0)))
            kf = k_ref[pl.ds(r0, bk), :].astype(F32)
            k1_ref[pl.ds(r0, bk), :] = jnp.where(lane < dh, kf, feat).astype(BF16)
            k2_ref[pl.ds(r0, bk), :] = jnp.where(lane < dh, pltpu.roll(kf, dh, 1), feat).astype(BF16)
            vp_ref[pl.ds(r0, bk), 0:128] = v_ref[pl.ds(r0, bk), :]
            vp_ref[pl.ds(r0, bk), 128:256] = jnp.ones((bk, 128), BF16)
            return carry

        lax.fori_loop(0, S // bk, build, 0)
        i2 = lax.broadcasted_iota(jnp.int32, (bq, bk), 0)
        for d in range(2):
            j2 = lax.broadcasted_iota(jnp.int32, (bq, bk), 1) + d * bk
            fut = jnp.where(j2 > i2, (-2.0 * LOG2E * slope) * (j2 - i2).astype(F32), 0.0)
            cd_ref[d] = jnp.where((j2 >> 6) <= (i2 >> 6), fut, NEG_BIG)

    lane = lax.broadcasted_iota(jnp.int32, (bq, 128), 1)
    c = (-LOG2E * (qi * bq).astype(F32)) * slope
    c1 = c.astype(BF16).astype(F32)
    c2 = (c - c1).astype(BF16).astype(F32)
    c3 = ((c - c1) - c2).astype(BF16).astype(F32)
    l1, l2, l3 = LOG2E_PIECES
    qfeat = jnp.where((lane == dh) | (lane == dh + 3), l1,
                      jnp.where((lane == dh + 1) | (lane == dh + 4), l2,
                                jnp.where((lane == dh + 2) | (lane == dh + 5), l3,
                                          jnp.where(lane == dh + 6, c1,
                                                    jnp.where(lane == dh + 7, c2,
                                                              jnp.where(lane == dh + 8, c3, 0.0))))))
    qf = q_ref[...].astype(F32) * (LOG2E * dh ** -0.5)
    qa_ref[0] = jnp.where(lane < dh, qf, qfeat).astype(BF16)
    qa_ref[1] = jnp.where(lane < dh, pltpu.roll(qf, dh, 1), qfeat).astype(BF16)
    kas = (k1_ref, k2_ref)

    m_ref[...] = jnp.full(m_ref.shape, NEG_BIG, F32)
    acc_ref[...] = jnp.zeros(acc_ref.shape, F32)

    def scores(blk, corr, slot):
        k0 = pl.multiple_of(blk * bk, bk)
        for c_ in range(2):
            sc = _dot_nt(qa_ref[c_], kas[c_][pl.ds(k0, bk), :])
            if corr is not None:
                sc = sc + cd_ref[corr]
            s_refs[slot][c_] = sc
            mc_refs[slot][c_] = jnp.broadcast_to(jnp.max(sc, axis=-1, keepdims=True), (bq, 128))

    def update(blk, slot):
        vb = vp_ref[pl.ds(pl.multiple_of(blk * bk, bk), bk), :]
        for c_ in range(2):
            m_prev = m_ref[c_]
            m_next = jnp.maximum(m_prev, mc_refs[slot][c_])
            p = jnp.exp2(s_refs[slot][c_] - jnp.concatenate([m_next] * (bk // 128), axis=1))
            alpha = jnp.exp2(m_prev - m_next)
            acc_ref[c_] = (jnp.concatenate([alpha, alpha], axis=1) * acc_ref[c_]
                           + _dot(p.astype(BF16), vb))
            m_ref[c_] = m_next

    last = jnp.maximum(2 * qi - 1, 0)
    scores(2 * qi, 0, 0)
    scores(2 * qi + 1, 1, 1)
    update(2 * qi, 0)
    scores(0, None, 0)
    update(2 * qi + 1, 1)

    def body(pr, carry):
        t = 2 * pr
        scores(t - 1, None, 1)
        update(t - 2, 0)
        scores(jnp.minimum(t, last), None, 0)
        update(t - 1, 1)
        return carry

    lax.fori_loop(1, qi + 1, body, 0)

    lm = lam_ref[...]
    lam = (jnp.exp(jnp.sum(lm[0:1, :] * lm[1:2, :], axis=-1, keepdims=True))
           - jnp.exp(jnp.sum(lm[2:3, :] * lm[3:4, :], axis=-1, keepdims=True)) + lam_init)
    a0 = acc_ref[0]
    a1 = acc_ref[1]
    o = a0[:, 0:128] / a0[:, 128:256] - lam * (a1[:, 0:128] / a1[:, 128:256])
    out_ref[...] = ((_rms(o) * sub_ref[...]) * (1.0 - lam_init)).astype(BF16)


def _dattn(proj, lam_p, subln, l, B, S, bq, lam_init):
    T = B * S
    nq = S // bq
    bk = bq // 2
    W = DIFF_HEADS * 2 * DIFF_HD
    return pl.pallas_call(
        functools.partial(_dattn_kernel, bq=bq, bk=bk, lam_init=lam_init),
        grid=(B, DIFF_HEADS, nq),
        in_specs=[
            pl.BlockSpec((bq, 128), lambda b, h, i: (b * nq + i, 32 + h)),
            pl.BlockSpec((S, 128), lambda b, h, i: (b, 40 + h)),
            pl.BlockSpec((S, 128), lambda b, h, i: (b, 48 + h)),
            pl.BlockSpec((None, 4, DIFF_HD), lambda b, h, i: (l, 0, 0)),
            pl.BlockSpec((None, 1, 128), lambda b, h, i: (l, 0, 0)),
        ],
        out_specs=pl.BlockSpec((bq, 128), lambda b, h, i: (b * nq + i, h)),
        out_shape=jax.ShapeDtypeStruct((T, W), BF16),
        scratch_shapes=[
            pltpu.VMEM((S, 128), BF16),
            pltpu.VMEM((S, 128), BF16),
            pltpu.VMEM((S, 256), BF16),
            pltpu.VMEM((2, bq, bk), F32),
            pltpu.VMEM((2, bq, 256), F32),
            pltpu.VMEM((2, bq, 128), F32),
            pltpu.VMEM((2, bq, 128), BF16),
            pltpu.VMEM((2, bq, bk), F32),
            pltpu.VMEM((2, bq, bk), F32),
            pltpu.VMEM((2, bq, 128), F32),
            pltpu.VMEM((2, bq, 128), F32),
        ],
        compiler_params=pltpu.CompilerParams(
            dimension_semantics=("arbitrary", "arbitrary", "arbitrary"),
            vmem_limit_bytes=VMEM_LIMIT),
        name="dattn",
    )(proj, proj, proj, lam_p, subln)


def _merge_kernel(hm_ref, hd_ref, hr_ref, gm_ref, gd_ref, gr_ref, x_ref,
                  wm_ref, wd_ref, wr_ref, wo_ref, g_ref, out_ref):
    y = _sigmoid(gm_ref[...].astype(F32)) * _dot(hm_ref[...], wm_ref[...])
    y = y + _sigmoid(gd_ref[...].astype(F32)) * _dot(hd_ref[...], wd_ref[...])
    y = y + _sigmoid(gr_ref[...].astype(F32)) * _dot(hr_ref[...], wr_ref[...])
    y2 = _dot(y.astype(BF16), wo_ref[...])
    out_ref[...] = x_ref[...] + _rms(y2) * g_ref[...]


def _merge(hm, hd, hr, proj, x2, wm, wd, wr, wo, gains, l, tm):
    T, D = x2.shape
    row = lambda c0: pl.BlockSpec((tm, D), lambda i: (i, c0))
    wspec = pl.BlockSpec((None, D, D), lambda i: (l, 0, 0))
    return pl.pallas_call(
        _merge_kernel,
        grid=(T // tm,),
        in_specs=[row(0), row(0), row(0), row(11), row(12), row(13), row(0),
                  wspec, wspec, wspec, wspec,
                  pl.BlockSpec((None, None, 1, D), lambda i: (l, 1, 0, 0))],
        out_specs=row(0),
        out_shape=jax.ShapeDtypeStruct((T, D), F32),
        compiler_params=pltpu.CompilerParams(
            dimension_semantics=("parallel",), vmem_limit_bytes=VMEM_LIMIT),
        name="merge",
    )(hm, hd, hr, proj, proj, proj, x2, wm, wd, wr, wo, gains)


def _ffn_kernel(x_ref, g2_ref, g3_ref, wa_ref, wu_ref, wo_ref, out_ref, *, hc):
    x = x_ref[...]
    h = (_rms(x) * g2_ref[...]).astype(BF16)
    f = None
    for c in range(FFN_HIDDEN // hc):
        sl = slice(c * hc, (c + 1) * hc)
        a = _dot(h, wa_ref[:, sl])
        u = _dot(h, wu_ref[:, sl])
        t = ((a * _sigmoid(a)) * u).astype(BF16)
        part = _dot(t, wo_ref[sl, :])
        f = part if f is None else f + part
    out_ref[...] = x + _rms(f) * g3_ref[...]


def _ffn(x2, gains, w_in, wo, l, tm, hc):
    T, D = x2.shape
    Hd = wo.shape[1]
    vec = lambda r: pl.BlockSpec((None, None, 1, D), lambda i: (l, r, 0, 0))
    return pl.pallas_call(
        functools.partial(_ffn_kernel, hc=hc),
        grid=(T // tm,),
        in_specs=[pl.BlockSpec((tm, D), lambda i: (i, 0)), vec(2), vec(3),
                  pl.BlockSpec((None, D, Hd), lambda i: (l, 0, 0)),
                  pl.BlockSpec((None, D, Hd), lambda i: (l, 0, 1)),
                  pl.BlockSpec((None, Hd, D), lambda i: (l, 0, 0))],
        out_specs=pl.BlockSpec((tm, D), lambda i: (i, 0)),
        out_shape=jax.ShapeDtypeStruct((T, D), F32),
        compiler_params=pltpu.CompilerParams(
            dimension_semantics=("parallel",), vmem_limit_bytes=VMEM_LIMIT),
        name="ffn",
    )(x2, gains, gains, w_in, w_in, wo)


def _tiles(B, S):
    T = B * S
    return dict(
        tm_in=min(2048, T), tn_in=1024,
        lc=min(256, S), bq=min(1024, S),
        tm_merge=min(512, T), tm_ffn=min(512, T), hc=1408,
    )


def kernel(x, norm_gains, w_in, mlstm_conv, mlstm_gate_bias, diff_lambda, diff_subln,
           w_mlstm_out, w_diff_out, w_ret_out, w_out, w_ffn_in, w_ffn_out):
    B, S, D = x.shape
    depth = w_in.shape[0]
    T = B * S
    cfg = _tiles(B, S)
    x2 = x.reshape(T, D).astype(F32)
    W = MLSTM_HEADS * MLSTM_HD

    gains = norm_gains.astype(F32).reshape(depth, 4, 1, D)
    w_main = jnp.concatenate([w_in[:, :, :4 * W], w_in[:, :, 4 * W + 8:]], axis=2).astype(BF16)
    w_if = jnp.pad(w_in[:, :, 4 * W:4 * W + 8], ((0, 0), (0, 0), (0, GATE_PAD - 8))).astype(BF16)
    gate_bias = jnp.pad(mlstm_gate_bias.astype(F32).reshape(depth, 1, 8), ((0, 0), (0, 0), (0, GATE_PAD - 8)))
    conv_w = mlstm_conv.astype(F32)
    lam_p = diff_lambda.astype(F32)
    subln = diff_subln.astype(F32).reshape(depth, 1, 2 * DIFF_HD)
    wm, wd, wr, wo = (w.astype(BF16) for w in (w_mlstm_out, w_diff_out, w_ret_out, w_out))
    wf_in, wf_out = w_ffn_in.astype(BF16), w_ffn_out.astype(BF16)

    for l in range(depth):
        proj, gates = _inproj(x2, gains, w_main, w_if, l, cfg["tm_in"], cfg["tn_in"])
        hm = _mlstm(proj, gates, gate_bias, conv_w, l, B, S, cfg["lc"])
        lam_init = 0.8 - 0.6 * math.exp(-0.3 * l)
        hd = _dattn(proj, lam_p, subln, l, B, S, cfg["bq"], lam_init)
        hr = _retention(proj, B, S, cfg["lc"])
        x2 = _merge(hm, hd, hr, proj, x2, wm, wd, wr, wo, gains, l, cfg["tm_merge"])
        x2 = _ffn(x2, gains, wf_in, wf_out, l, cfg["tm_ffn"], cfg["hc"])

    return x2.reshape(B, S, D).astype(x.dtype)
```

```python
import functools
import math

import numpy as np
import jax
import jax.numpy as jnp
from jax import lax
from jax.experimental import pallas as pl
from jax.experimental.pallas import tpu as pltpu

F32 = jnp.float32
BF16 = jnp.bfloat16

EPS = 1e-6
CHUNK = 64
D_MODEL = 1024
MLSTM_HEADS = 4
MLSTM_HD = 256
DIFF_HEADS = 8
DIFF_HD = 64
RET_HEADS = 4
RET_HD = 256
CONV_K = 4
FFN_HIDDEN = 2816
N_MAIN = 14 * D_MODEL
GATE_PAD = 128
MXU_TILE = 256
NEG_BIG = -1e30
LOG2E = math.log2(math.e)


def _bf16_pieces(x, n=3):
    out, r = [], np.float64(x)
    for _ in range(n):
        piece = float(np.asarray(r, dtype=jnp.bfloat16).astype(np.float64))
        out.append(piece)
        r = r - piece
    return tuple(out)


LOG2E_PIECES = _bf16_pieces(LOG2E)

VMEM_LIMIT = 56 * 1024 * 1024


def _dot(a, b):
    return jnp.dot(a, b, preferred_element_type=F32)


def _dot_nt(a, b):
    return lax.dot_general(a, b, (((1,), (1,)), ((), ())), preferred_element_type=F32)


def _dot_tn(a, b):
    return lax.dot_general(a, b, (((0,), (0,)), ((), ())), preferred_element_type=F32)


def _sigmoid(x):
    return 1.0 / (1.0 + jnp.exp(-x))


def _rms(x, axis=-1):
    return x * lax.rsqrt(jnp.mean(x * x, axis=axis, keepdims=True) + EPS)


def _inproj_kernel(x_ref, g_ref, w_ref, wif_ref, o_ref, oif_ref, h_ref):
    @pl.when(pl.program_id(1) == 0)
    def _():
        x = x_ref[...]
        h = (_rms(x) * g_ref[...]).astype(BF16)
        h_ref[...] = h
        oif_ref[...] = _dot(h, wif_ref[...])

    o_ref[...] = _dot(h_ref[...], w_ref[...]).astype(BF16)


def _inproj(x2, gains, w_main, w_if, l, tm, tn):
    T, D = x2.shape
    n = w_main.shape[2]
    return pl.pallas_call(
        _inproj_kernel,
        grid=(T // tm, n // tn),
        in_specs=[
            pl.BlockSpec((tm, D), lambda i, j: (i, 0)),
            pl.BlockSpec((None, None, 1, D), lambda i, j: (l, 0, 0, 0)),
            pl.BlockSpec((None, D, tn), lambda i, j: (l, 0, j)),
            pl.BlockSpec((None, D, GATE_PAD), lambda i, j: (l, 0, 0)),
        ],
        out_specs=[
            pl.BlockSpec((tm, tn), lambda i, j: (i, j)),
            pl.BlockSpec((tm, GATE_PAD), lambda i, j: (i, 0)),
        ],
        out_shape=[
            jax.ShapeDtypeStruct((T, n), BF16),
            jax.ShapeDtypeStruct((T, GATE_PAD), F32),
        ],
        scratch_shapes=[pltpu.VMEM((tm, D), BF16)],
        compiler_params=pltpu.CompilerParams(
            dimension_semantics=("parallel", "arbitrary"), vmem_limit_bytes=VMEM_LIMIT),
        name="inproj",
    )(x2, gains, w_main, w_if)


def _split3(x):
    a = x.astype(BF16)
    r = x - a.astype(F32)
    b = r.astype(BF16)
    c = (r - b.astype(F32)).astype(BF16)
    return a, b, c


def _mlstm_step(q_ref, k_ref, v_ref, o_ref, gt_ref, gb_ref, cw_ref, out_ref,
                xq_ref, xk_ref, c_ref, n_ref, m_ref, *, lc):
    H, dh = MLSTM_HEADS, MLSTM_HD
    W = H * dh

    row = lax.broadcasted_iota(jnp.int32, (lc, lc), 0)
    col = lax.broadcasted_iota(jnp.int32, (lc, lc), 1)
    causal = row >= col
    rid = lax.broadcasted_iota(jnp.int32, (8, W), 0)
    shifts = [jnp.where(row - col == j, 1.0, 0.0).astype(BF16) for j in range(1, CONV_K)]

    def conv_silu(x_ref, tail_ref, w):
        xb = x_ref[...]
        z = xb.astype(F32) * w[3:4, :]
        for j in range(1, CONV_K):
            z = z + _dot(shifts[j - 1], xb) * w[3 - j:4 - j, :]
        tl = tail_ref[...]
        p1, p2, p3 = tl[7:8, :], tl[6:7, :], tl[5:6, :]
        head = jnp.where(rid == 0, p1 * w[2:3, :] + p2 * w[1:2, :] + p3 * w[0:1, :],
                         jnp.where(rid == 1, p1 * w[1:2, :] + p2 * w[0:1, :],
                                   jnp.where(rid == 2, p1 * w[0:1, :], 0.0)))
        z = z + jnp.concatenate([head, jnp.zeros((lc - 8, W), F32)], axis=0)
        tail_ref[...] = xb[lc - 8:lc, :].astype(F32)
        return z * _sigmoid(z)

    cw = cw_ref[...]
    qc = conv_silu(q_ref, xq_ref, cw[:, 0:W])
    kc = conv_silu(k_ref, xk_ref, cw[:, W:2 * W]) * (dh ** -0.5)

    g = gt_ref[...] + gb_ref[...]
    logf = jnp.minimum(g, 0.0) - jnp.log(1.0 + jnp.exp(-jnp.abs(g)))
    tril = jnp.where(causal, 1.0, 0.0).astype(BF16)
    f1, f2, f3 = _split3(logf)
    bc = _dot(tril, f1) + _dot(tril, f2) + _dot(tril, f3)
    bt = bc.T
    gtr = g.T

    for h in range(H):
        sl = slice(h * dh, (h + 1) * dh)
        qh = qc[:, sl]
        kh = kc[:, sl]
        qb = qh.astype(BF16)
        kb = kh.astype(BF16)
        vb = v_ref[:, sl]
        b_col = bc[:, 4 + h:5 + h]
        i_col = g[:, h:h + 1]
        b_row = bt[4 + h:5 + h, :]
        i_row = gtr[h:h + 1, :]
        gsum = bc[lc - 1:lc, 4 + h:5 + h]
        m_old = m_ref[h:h + 1, 0:1]
        c_old = c_ref[h]
        n_old = n_ref[h:h + 1, :]

        d = jnp.where(causal, b_col - b_row + i_row, NEG_BIG)
        d_inter = b_col + m_old
        m_row = jnp.maximum(jnp.max(d, axis=-1, keepdims=True), d_inter)
        w_intra = jnp.exp(d - m_row)
        w_inter = jnp.exp(d_inter - m_row)
        s = _dot_nt(qb, kb) * w_intra
        num = _dot(s.astype(BF16), vb) + _dot(qb, c_old.astype(BF16)) * w_inter
        den = (jnp.sum(s, axis=-1, keepdims=True)
               + jnp.sum(qh * n_old, axis=-1, keepdims=True) * w_inter)
        den = jnp.maximum(jnp.abs(den), jnp.exp(-m_row))
        hh = num / den
        og = o_ref[:, sl].astype(F32)
        out_ref[:, sl] = (_sigmoid(og) * _rms(hh)).astype(BF16)

        a_col = gsum - b_col + i_col
        m_new = jnp.maximum(gsum + m_old, jnp.max(a_col, axis=0, keepdims=True))
        w_old = jnp.exp(gsum + m_old - m_new)
        kw = kh * jnp.exp(a_col - m_new)
        c_ref[h] = w_old * c_old + _dot_tn(kw.astype(BF16), vb)
        n_ref[h:h + 1, :] = w_old * n_old + jnp.sum(kw, axis=0, keepdims=True)
        m_ref[h:h + 1, :] = jnp.broadcast_to(m_new, (1, 128))


def _ret_decay_masks(dm_ref, lc):
    row = lax.broadcasted_iota(jnp.int32, (lc, lc), 0)
    col = lax.broadcasted_iota(jnp.int32, (lc, lc), 1)
    rel = (row - col).astype(F32)
    for h in range(RET_HEADS):
        lg = math.log(1.0 - 2.0 ** (-5.0 - h))
        dm_ref[h] = jnp.where(rel >= 0.0, jnp.exp(lg * jnp.maximum(rel, 0.0)), 0.0)


def _ret_step(q_ref, k_ref, v_ref, g_ref, out_ref, r_ref, dm_ref, *, lc):
    H, dh = RET_HEADS, RET_HD
    pos = lax.broadcasted_iota(jnp.int32, (lc, 1), 0).astype(F32)
    for h in range(H):
        lg = math.log(1.0 - 2.0 ** (-5.0 - h))
        sl = slice(h * dh, (h + 1) * dh)
        qb = q_ref[:, sl]
        kf = k_ref[:, sl].astype(F32) * (dh ** -0.5)
        vb = v_ref[:, sl]
        r_old = r_ref[h]
        xi = jnp.exp(lg * (pos + 1.0))
        zeta = jnp.exp(lg * (lc - 1.0 - pos))
        s = _dot_nt(qb, kf.astype(BF16)) * dm_ref[h]
        o = _dot(s.astype(BF16), vb) + _dot(qb, r_old.astype(BF16)) * xi
        r_ref[h] = math.exp(lg * lc) * r_old + _dot_tn((kf * zeta).astype(BF16), vb)
        gg = g_ref[:, sl].astype(F32)
        out_ref[:, sl] = (gg * _sigmoid(gg) * _rms(o)).astype(BF16)


def _scan_kernel(mq_ref, mk_ref, mv_ref, mo_ref, gt_ref, gb_ref, cw_ref, rq_ref, rk_ref, rv_ref, rg_ref,
                 hm_ref, hr_ref, xq_ref, xk_ref, c_ref, n_ref, m_ref, r_ref, dm_ref, *, lc):
    @pl.when(pl.program_id(1) == 0)
    def _():
        for ref in (xq_ref, xk_ref, c_ref, n_ref, m_ref, r_ref):
            ref[...] = jnp.zeros_like(ref)

    @pl.when((pl.program_id(0) == 0) & (pl.program_id(1) == 0))
    def _():
        _ret_decay_masks(dm_ref, lc)

    _mlstm_step(mq_ref, mk_ref, mv_ref, mo_ref, gt_ref, gb_ref, cw_ref, hm_ref,
                xq_ref, xk_ref, c_ref, n_ref, m_ref, lc=lc)
    _ret_step(rq_ref, rk_ref, rv_ref, rg_ref, hr_ref, r_ref, dm_ref, lc=lc)


def _scans(proj, gates, gate_bias, conv_w, l, B, S, lc):
    T = B * S
    nc = S // lc
    W = MLSTM_HEADS * MLSTM_HD
    blk = lambda c0: pl.BlockSpec((lc, W), lambda b, c: (b * nc + c, c0))
    return pl.pallas_call(
        functools.partial(_scan_kernel, lc=lc),
        grid=(B, nc),
        in_specs=[
            blk(0), blk(1), blk(2), blk(3),
            pl.BlockSpec((lc, GATE_PAD), lambda b, c: (b * nc + c, 0)),
            pl.BlockSpec((None, 1, GATE_PAD), lambda b, c: (l, 0, 0)),
            pl.BlockSpec((None, CONV_K, 2 * W), lambda b, c: (l, 0, 0)),
            blk(7), blk(8), blk(9), blk(10),
        ],
        out_specs=[blk(0), blk(0)],
        out_shape=[jax.ShapeDtypeStruct((T, W), BF16), jax.ShapeDtypeStruct((T, W), BF16)],
        scratch_shapes=[
            pltpu.VMEM((8, W), F32),
            pltpu.VMEM((8, W), F32),
            pltpu.VMEM((MLSTM_HEADS, MLSTM_HD, MLSTM_HD), F32),
            pltpu.VMEM((8, MLSTM_HD), F32),
            pltpu.VMEM((8, 128), F32),
            pltpu.VMEM((RET_HEADS, RET_HD, RET_HD), F32),
            pltpu.VMEM((RET_HEADS, lc, lc), F32),
        ],
        compiler_params=pltpu.CompilerParams(
            dimension_semantics=("arbitrary", "arbitrary"), vmem_limit_bytes=VMEM_LIMIT),
        name="scans",
    )(proj, proj, proj, proj, gates, gate_bias, conv_w, proj, proj, proj, proj)


def _dattn_kernel(q_ref, k_ref, v_ref, lam_ref, sub_ref, out_ref,
                  k1_ref, k2_ref, vp_ref, cd_ref, acc_ref, m_ref, qa_ref, s0_ref, s1_ref, mc0_ref, mc1_ref,
                  *, bq, bk, lam_init):
    s_refs = (s0_ref, s1_ref)
    mc_refs = (mc0_ref, mc1_ref)
    dh = DIFF_HD
    S = k_ref.shape[0]
    h = pl.program_id(1)
    qi = pl.program_id(2)
    slope = lax.bitcast_convert_type(jnp.full((1, 1), (126 - h) << 23, jnp.int32), F32)

    @pl.when(qi == 0)
    def _():
        lane = lax.broadcasted_iota(jnp.int32, (bk, 128), 1)
        rloc = lax.broadcasted_iota(jnp.int32, (bk, 128), 0)

        def build(i, carry):
            r0 = pl.multiple_of(i * bk, bk)
            t = rloc + r0
            fa = slope * (t - (t & (CHUNK - 1))).astype(F32)
            fb = slope * (t & (CHUNK - 1)).astype(F32)
            feat = jnp.where(lane < dh + 3, fa, jnp.where(lane < dh + 6, fb,
                                                          jnp.where(lane < dh + 9, 1.0, 0.0)))
            kf = k_ref[pl.ds(r0, bk), :].astype(F32)
            k1_ref[pl.ds(r0, bk), :] = jnp.where(lane < dh, kf, feat).astype(BF16)
            k2_ref[pl.ds(r0, bk), :] = jnp.where(lane < dh, pltpu.roll(kf, dh, 1), feat).astype(BF16)
            vp_ref[pl.ds(r0, bk), 0:128] = v_ref[pl.ds(r0, bk), :]
            vp_ref[pl.ds(r0, bk), 128:256] = jnp.ones((bk, 128), BF16)
            return carry

        lax.fori_loop(0, S // bk, build, 0)
        i2 = lax.broadcasted_iota(jnp.int32, (bq, bk), 0)
        for d in range(2):
            j2 = lax.broadcasted_iota(jnp.int32, (bq, bk), 1) + d * bk
            fut = jnp.where(j2 > i2, (-2.0 * LOG2E * slope) * (j2 - i2).astype(F32), 0.0)
            cd_ref[d] = jnp.where((j2 >> 6) <= (i2 >> 6), fut, NEG_BIG)

    lane = lax.broadcasted_iota(jnp.int32, (bq, 128), 1)
    c = (-LOG2E * (qi * bq).astype(F32)) * slope
    c1 = c.astype(BF16).astype(F32)
    c2 = (c - c1).astype(BF16).astype(F32)
    c3 = ((c - c1) - c2).astype(BF16).astype(F32)
    l1, l2, l3 = LOG2E_PIECES
    qfeat = jnp.where((lane == dh) | (lane == dh + 3), l1,
                      jnp.where((lane == dh + 1) | (lane == dh + 4), l2,
                                jnp.where((lane == dh + 2) | (lane == dh + 5), l3,
                                          jnp.where(lane == dh + 6, c1,
                                                    jnp.where(lane == dh + 7, c2,
                                                              jnp.where(lane == dh + 8, c3, 0.0))))))
    qf = q_ref[...].astype(F32) * (LOG2E * dh ** -0.5)
    qa_ref[0] = jnp.where(lane < dh, qf, qfeat).astype(BF16)
    qa_ref[1] = jnp.where(lane < dh, pltpu.roll(qf, dh, 1), qfeat).astype(BF16)
    kas = (k1_ref, k2_ref)

    m_ref[...] = jnp.full(m_ref.shape, NEG_BIG, F32)
    acc_ref[...] = jnp.zeros(acc_ref.shape, F32)

    def scores(blk, corr, slot):
        k0 = pl.multiple_of(blk * bk, bk)
        for c_ in range(2):
            sc = _dot_nt(qa_ref[c_], kas[c_][pl.ds(k0, bk), :])
            if corr is not None:
                sc = sc + cd_ref[corr]
            s_refs[slot][c_] = sc
            mc_refs[slot][c_] = jnp.broadcast_to(jnp.max(sc, axis=-1, keepdims=True), (bq, 128))

    def update(blk, slot):
        vb = vp_ref[pl.ds(pl.multiple_of(blk * bk, bk), bk), :]
        for c_ in range(2):
            m_prev = m_ref[c_]
            m_next = jnp.maximum(m_prev, mc_refs[slot][c_])
            p = jnp.exp2(s_refs[slot][c_] - jnp.concatenate([m_next] * (bk // 128), axis=1))
            alpha = jnp.exp2(m_prev - m_next)
            acc_ref[c_] = (jnp.concatenate([alpha, alpha], axis=1) * acc_ref[c_]
                           + _dot(p.astype(BF16), vb))
            m_ref[c_] = m_next

    last = jnp.maximum(2 * qi - 1, 0)
    scores(2 * qi, 0, 0)
    scores(2 * qi + 1, 1, 1)
    update(2 * qi, 0)
    scores(0, None, 0)
    update(2 * qi + 1, 1)

    def body(pr, carry):
        t = 2 * pr
        scores(t - 1, None, 1)
        update(t - 2, 0)
        scores(jnp.minimum(t, last), None, 0)
        update(t - 1, 1)
        return carry

    lax.fori_loop(1, qi + 1, body, 0)

    lm = lam_ref[...]
    lam = (jnp.exp(jnp.sum(lm[0:1, :] * lm[1:2, :], axis=-1, keepdims=True))
           - jnp.exp(jnp.sum(lm[2:3, :] * lm[3:4, :], axis=-1, keepdims=True)) + lam_init)
    a0 = acc_ref[0]
    a1 = acc_ref[1]
    o = a0[:, 0:128] / a0[:, 128:256] - lam * (a1[:, 0:128] / a1[:, 128:256])
    out_ref[...] = ((_rms(o) * sub_ref[...]) * (1.0 - lam_init)).astype(BF16)


def _dattn(proj, lam_p, subln, l, B, S, bq, lam_init):
    T = B * S
    nq = S // bq
    bk = bq // 2
    W = DIFF_HEADS * 2 * DIFF_HD
    return pl.pallas_call(
        functools.partial(_dattn_kernel, bq=bq, bk=bk, lam_init=lam_init),
        grid=(B, DIFF_HEADS, nq),
        in_specs=[
            pl.BlockSpec((bq, 128), lambda b, h, i: (b * nq + i, 32 + h)),
            pl.BlockSpec((S, 128), lambda b, h, i: (b, 40 + h)),
            pl.BlockSpec((S, 128), lambda b, h, i: (b, 48 + h)),
            pl.BlockSpec((None, 4, DIFF_HD), lambda b, h, i: (l, 0, 0)),
            pl.BlockSpec((None, 1, 128), lambda b, h, i: (l, 0, 0)),
        ],
        out_specs=pl.BlockSpec((bq, 128), lambda b, h, i: (b * nq + i, h)),
        out_shape=jax.ShapeDtypeStruct((T, W), BF16),
        scratch_shapes=[
            pltpu.VMEM((S, 128), BF16),
            pltpu.VMEM((S, 128), BF16),
            pltpu.VMEM((S, 256), BF16),
            pltpu.VMEM((2, bq, bk), F32),
            pltpu.VMEM((2, bq, 256), F32),
            pltpu.VMEM((2, bq, 128), F32),
            pltpu.VMEM((2, bq, 128), BF16),
            pltpu.VMEM((2, bq, bk), F32),
            pltpu.VMEM((2, bq, bk), F32),
            pltpu.VMEM((2, bq, 128), F32),
            pltpu.VMEM((2, bq, 128), F32),
        ],
        compiler_params=pltpu.CompilerParams(
            dimension_semantics=("arbitrary", "arbitrary", "arbitrary"),
            vmem_limit_bytes=VMEM_LIMIT),
        name="dattn",
    )(proj, proj, proj, lam_p, subln)


def _merge_kernel(hm_ref, hd_ref, hr_ref, gm_ref, gd_ref, gr_ref, x_ref,
                  wm_ref, wd_ref, wr_ref, wo_ref, g_ref, out_ref):
    y = _sigmoid(gm_ref[...].astype(F32)) * _dot(hm_ref[...], wm_ref[...])
    y = y + _sigmoid(gd_ref[...].astype(F32)) * _dot(hd_ref[...], wd_ref[...])
    y = y + _sigmoid(gr_ref[...].astype(F32)) * _dot(hr_ref[...], wr_ref[...])
    y2 = _dot(y.astype(BF16), wo_ref[...])
    out_ref[...] = x_ref[...] + _rms(y2) * g_ref[...]


def _merge(hm, hd, hr, proj, x2, wm, wd, wr, wo, gains, l, tm):
    T, D = x2.shape
    row = lambda c0: pl.BlockSpec((tm, D), lambda i: (i, c0))
    wspec = pl.BlockSpec((None, D, D), lambda i: (l, 0, 0))
    return pl.pallas_call(
        _merge_kernel,
        grid=(T // tm,),
        in_specs=[row(0), row(0), row(0), row(11), row(12), row(13), row(0),
                  wspec, wspec, wspec, wspec,
                  pl.BlockSpec((None, None, 1, D), lambda i: (l, 1, 0, 0))],
        out_specs=row(0),
        out_shape=jax.ShapeDtypeStruct((T, D), F32),
        compiler_params=pltpu.CompilerParams(
            dimension_semantics=("parallel",), vmem_limit_bytes=VMEM_LIMIT),
        name="merge",
    )(hm, hd, hr, proj, proj, proj, x2, wm, wd, wr, wo, gains)


def _ffn_kernel(x_ref, g2_ref, g3_ref, wa_ref, wu_ref, wo_ref, out_ref, *, hc):
    x = x_ref[...]
    h = (_rms(x) * g2_ref[...]).astype(BF16)
    f = None
    for c0 in range(0, FFN_HIDDEN, hc):
        sl = slice(c0, min(c0 + hc, FFN_HIDDEN))
        a = _dot(h, wa_ref[:, sl])
        u = _dot(h, wu_ref[:, sl])
        t = ((a * _sigmoid(a)) * u).astype(BF16)
        part = _dot(t, wo_ref[sl, :])
        f = part if f is None else f + part
    out_ref[...] = x + _rms(f) * g3_ref[...]


def _ffn(x2, gains, w_in, wo, l, tm, hc):
    T, D = x2.shape
    Hd = wo.shape[1]
    vec = lambda r: pl.BlockSpec((None, None, 1, D), lambda i: (l, r, 0, 0))
    return pl.pallas_call(
        functools.partial(_ffn_kernel, hc=hc),
        grid=(T // tm,),
        in_specs=[pl.BlockSpec((tm, D), lambda i: (i, 0)), vec(2), vec(3),
                  pl.BlockSpec((None, D, Hd), lambda i: (l, 0, 0)),
                  pl.BlockSpec((None, D, Hd), lambda i: (l, 0, 1)),
                  pl.BlockSpec((None, Hd, D), lambda i: (l, 0, 0))],
        out_specs=pl.BlockSpec((tm, D), lambda i: (i, 0)),
        out_shape=jax.ShapeDtypeStruct((T, D), F32),
        compiler_params=pltpu.CompilerParams(
            dimension_semantics=("parallel",), vmem_limit_bytes=VMEM_LIMIT),
        name="ffn",
    )(x2, gains, gains, w_in, w_in, wo)


def _tiles(B, S):
    T = B * S
    return dict(
        tm_in=min(2048, T), tn_in=1024,
        lc=min(256, S), bq=min(1024, S),
        tm_merge=min(512, T), tm_ffn=min(512, T), hc=6 * MXU_TILE,
    )


def kernel(x, norm_gains, w_in, mlstm_conv, mlstm_gate_bias, diff_lambda, diff_subln,
           w_mlstm_out, w_diff_out, w_ret_out, w_out, w_ffn_in, w_ffn_out):
    B, S, D = x.shape
    depth = w_in.shape[0]
    T = B * S
    cfg = _tiles(B, S)
    x2 = x.reshape(T, D).astype(F32)
    W = MLSTM_HEADS * MLSTM_HD

    gains = norm_gains.astype(F32).reshape(depth, 4, 1, D)
    w_main = jnp.concatenate([w_in[:, :, :4 * W], w_in[:, :, 4 * W + 8:]], axis=2).astype(BF16)
    w_if = jnp.pad(w_in[:, :, 4 * W:4 * W + 8], ((0, 0), (0, 0), (0, GATE_PAD - 8))).astype(BF16)
    gate_bias = jnp.pad(mlstm_gate_bias.astype(F32).reshape(depth, 1, 8), ((0, 0), (0, 0), (0, GATE_PAD - 8)))
    conv_w = mlstm_conv.astype(F32)
    lam_p = diff_lambda.astype(F32)
    subln = diff_subln.astype(F32).reshape(depth, 1, 2 * DIFF_HD)
    wm, wd, wr, wo = (w.astype(BF16) for w in (w_mlstm_out, w_diff_out, w_ret_out, w_out))
    wf_in, wf_out = w_ffn_in.astype(BF16), w_ffn_out.astype(BF16)

    for l in range(depth):
        proj, gates = _inproj(x2, gains, w_main, w_if, l, cfg["tm_in"], cfg["tn_in"])
        hm, hr = _scans(proj, gates, gate_bias, conv_w, l, B, S, cfg["lc"])
        lam_init = 0.8 - 0.6 * math.exp(-0.3 * l)
        hd = _dattn(proj, lam_p, subln, l, B, S, cfg["bq"], lam_init)
        x2 = _merge(hm, hd, hr, proj, x2, wm, wd, wr, wo, gains, l, cfg["tm_merge"])
        x2 = _ffn(x2, gains, wf_in, wf_out, l, cfg["tm_ffn"], cfg["hc"])

    return x2.reshape(B, S, D).astype(x.dtype)
```

```python
import functools
import math

import numpy as np
import jax
import jax.numpy as jnp
from jax import lax
from jax.experimental import pallas as pl
from jax.experimental.pallas import tpu as pltpu

F32 = jnp.float32
BF16 = jnp.bfloat16

EPS = 1e-6
CHUNK = 64
D_MODEL = 1024
MLSTM_HEADS = 4
MLSTM_HD = 256
DIFF_HEADS = 8
DIFF_HD = 64
RET_HEADS = 4
RET_HD = 256
CONV_K = 4
FFN_HIDDEN = 2816
N_MAIN = 14 * D_MODEL
GATE_PAD = 128
MXU_TILE = 256
NEG_BIG = -1e30
LOG2E = math.log2(math.e)


def _bf16_pieces(x, n=3):
    out, r = [], np.float64(x)
    for _ in range(n):
        piece = float(np.asarray(r, dtype=jnp.bfloat16).astype(np.float64))
        out.append(piece)
        r = r - piece
    return tuple(out)


LOG2E_PIECES = _bf16_pieces(LOG2E)

VMEM_LIMIT = 56 * 1024 * 1024


def _dot(a, b):
    return jnp.dot(a, b, preferred_element_type=F32)


def _dot_nt(a, b):
    return lax.dot_general(a, b, (((1,), (1,)), ((), ())), preferred_element_type=F32)


def _dot_tn(a, b):
    return lax.dot_general(a, b, (((0,), (0,)), ((), ())), preferred_element_type=F32)


def _sigmoid(x):
    return 1.0 / (1.0 + jnp.exp(-x))


def _rms(x, axis=-1):
    return x * lax.rsqrt(jnp.mean(x * x, axis=axis, keepdims=True) + EPS)


def _inproj_kernel(x_ref, g_ref, w_ref, wif_ref, o_ref, oif_ref, h_ref):
    @pl.when(pl.program_id(1) == 0)
    def _():
        x = x_ref[...]
        h = (_rms(x) * g_ref[...]).astype(BF16)
        h_ref[...] = h
        oif_ref[...] = _dot(h, wif_ref[...])

    o_ref[...] = _dot(h_ref[...], w_ref[...]).astype(BF16)


def _inproj(x2, gains, w_main, w_if, l, tm, tn):
    T, D = x2.shape
    n = w_main.shape[2]
    return pl.pallas_call(
        _inproj_kernel,
        grid=(T // tm, n // tn),
        in_specs=[
            pl.BlockSpec((tm, D), lambda i, j: (i, 0)),
            pl.BlockSpec((None, None, 1, D), lambda i, j: (l, 0, 0, 0)),
            pl.BlockSpec((None, D, tn), lambda i, j: (l, 0, j)),
            pl.BlockSpec((None, D, GATE_PAD), lambda i, j: (l, 0, 0)),
        ],
        out_specs=[
            pl.BlockSpec((tm, tn), lambda i, j: (i, j)),
            pl.BlockSpec((tm, GATE_PAD), lambda i, j: (i, 0)),
        ],
        out_shape=[
            jax.ShapeDtypeStruct((T, n), BF16),
            jax.ShapeDtypeStruct((T, GATE_PAD), F32),
        ],
        scratch_shapes=[pltpu.VMEM((tm, D), BF16)],
        compiler_params=pltpu.CompilerParams(
            dimension_semantics=("parallel", "arbitrary"), vmem_limit_bytes=VMEM_LIMIT),
        name="inproj",
    )(x2, gains, w_main, w_if)


def _split3(x):
    a = x.astype(BF16)
    r = x - a.astype(F32)
    b = r.astype(BF16)
    c = (r - b.astype(F32)).astype(BF16)
    return a, b, c


def _mlstm_step(q_ref, k_ref, v_ref, o_ref, gt_ref, gb_ref, cw_ref, out_ref,
                xq_ref, xk_ref, c_ref, n_ref, m_ref, *, lc):
    H, dh = MLSTM_HEADS, MLSTM_HD
    W = H * dh

    row = lax.broadcasted_iota(jnp.int32, (lc, lc), 0)
    col = lax.broadcasted_iota(jnp.int32, (lc, lc), 1)
    causal = row >= col
    rid = lax.broadcasted_iota(jnp.int32, (8, W), 0)
    shifts = [jnp.where(row - col == j, 1.0, 0.0).astype(BF16) for j in range(1, CONV_K)]

    def conv_silu(x_ref, tail_ref, w):
        xb = x_ref[...]
        z = xb.astype(F32) * w[3:4, :]
        for j in range(1, CONV_K):
            z = z + _dot(shifts[j - 1], xb) * w[3 - j:4 - j, :]
        tl = tail_ref[...]
        p1, p2, p3 = tl[7:8, :], tl[6:7, :], tl[5:6, :]
        head = jnp.where(rid == 0, p1 * w[2:3, :] + p2 * w[1:2, :] + p3 * w[0:1, :],
                         jnp.where(rid == 1, p1 * w[1:2, :] + p2 * w[0:1, :],
                                   jnp.where(rid == 2, p1 * w[0:1, :], 0.0)))
        z = z + jnp.concatenate([head, jnp.zeros((lc - 8, W), F32)], axis=0)
        tail_ref[...] = xb[lc - 8:lc, :].astype(F32)
        return z * _sigmoid(z)

    cw = cw_ref[...]
    qc = conv_silu(q_ref, xq_ref, cw[:, 0:W])
    kc = conv_silu(k_ref, xk_ref, cw[:, W:2 * W]) * (dh ** -0.5)

    g = gt_ref[...] + gb_ref[...]
    logf = jnp.minimum(g, 0.0) - jnp.log(1.0 + jnp.exp(-jnp.abs(g)))
    tril = jnp.where(causal, 1.0, 0.0).astype(BF16)
    f1, f2, f3 = _split3(logf)
    bc = _dot(tril, f1) + _dot(tril, f2) + _dot(tril, f3)
    bt = bc.T
    gtr = g.T

    for h in range(H):
        sl = slice(h * dh, (h + 1) * dh)
        qh = qc[:, sl]
        kh = kc[:, sl]
        qb = qh.astype(BF16)
        kb = kh.astype(BF16)
        vb = v_ref[:, sl]
        b_col = bc[:, 4 + h:5 + h]
        i_col = g[:, h:h + 1]
        b_row = bt[4 + h:5 + h, :]
        i_row = gtr[h:h + 1, :]
        gsum = bc[lc - 1:lc, 4 + h:5 + h]
        m_old = m_ref[h:h + 1, 0:1]
        c_old = c_ref[h]
        n_old = n_ref[h:h + 1, :]

        d = jnp.where(causal, b_col - b_row + i_row, NEG_BIG)
        d_inter = b_col + m_old
        m_row = jnp.maximum(jnp.max(d, axis=-1, keepdims=True), d_inter)
        w_intra = jnp.exp(d - m_row)
        w_inter = jnp.exp(d_inter - m_row)
        s = _dot_nt(qb, kb) * w_intra
        num = _dot(s.astype(BF16), vb) + _dot(qb, c_old.astype(BF16)) * w_inter
        den = (jnp.sum(s, axis=-1, keepdims=True)
               + jnp.sum(qh * n_old, axis=-1, keepdims=True) * w_inter)
        den = jnp.maximum(jnp.abs(den), jnp.exp(-m_row))
        hh = num / den
        og = o_ref[:, sl].astype(F32)
        out_ref[:, sl] = (_sigmoid(og) * _rms(hh)).astype(BF16)

        a_col = gsum - b_col + i_col
        m_new = jnp.maximum(gsum + m_old, jnp.max(a_col, axis=0, keepdims=True))
        w_old = jnp.exp(gsum + m_old - m_new)
        kw = kh * jnp.exp(a_col - m_new)
        c_ref[h] = w_old * c_old + _dot_tn(kw.astype(BF16), vb)
        n_ref[h:h + 1, :] = w_old * n_old + jnp.sum(kw, axis=0, keepdims=True)
        m_ref[h:h + 1, :] = jnp.broadcast_to(m_new, (1, 128))


def _ret_decay_masks(dm_ref, lc):
    row = lax.broadcasted_iota(jnp.int32, (lc, lc), 0)
    col = lax.broadcasted_iota(jnp.int32, (lc, lc), 1)
    rel = (row - col).astype(F32)
    for h in range(RET_HEADS):
        lg = math.log(1.0 - 2.0 ** (-5.0 - h))
        dm_ref[h] = jnp.where(rel >= 0.0, jnp.exp(lg * jnp.maximum(rel, 0.0)), 0.0)


def _ret_step(q_ref, k_ref, v_ref, g_ref, out_ref, r_ref, dm_ref, *, lc):
    H, dh = RET_HEADS, RET_HD
    pos = lax.broadcasted_iota(jnp.int32, (lc, 1), 0).astype(F32)
    for h in range(H):
        lg = math.log(1.0 - 2.0 ** (-5.0 - h))
        sl = slice(h * dh, (h + 1) * dh)
        qb = q_ref[:, sl]
        kf = k_ref[:, sl].astype(F32) * (dh ** -0.5)
        vb = v_ref[:, sl]
        r_old = r_ref[h]
        xi = jnp.exp(lg * (pos + 1.0))
        zeta = jnp.exp(lg * (lc - 1.0 - pos))
        s = _dot_nt(qb, kf.astype(BF16)) * dm_ref[h]
        o = _dot(s.astype(BF16), vb) + _dot(qb, r_old.astype(BF16)) * xi
        r_ref[h] = math.exp(lg * lc) * r_old + _dot_tn((kf * zeta).astype(BF16), vb)
        gg = g_ref[:, sl].astype(F32)
        out_ref[:, sl] = (gg * _sigmoid(gg) * _rms(o)).astype(BF16)


def _scan_kernel(mq_ref, mk_ref, mv_ref, mo_ref, gt_ref, gb_ref, cw_ref, rq_ref, rk_ref, rv_ref, rg_ref,
                 hm_ref, hr_ref, xq_ref, xk_ref, c_ref, n_ref, m_ref, r_ref, dm_ref, *, lc):
    @pl.when(pl.program_id(1) == 0)
    def _():
        for ref in (xq_ref, xk_ref, c_ref, n_ref, m_ref, r_ref):
            ref[...] = jnp.zeros_like(ref)

    @pl.when((pl.program_id(0) == 0) & (pl.program_id(1) == 0))
    def _():
        _ret_decay_masks(dm_ref, lc)

    _mlstm_step(mq_ref, mk_ref, mv_ref, mo_ref, gt_ref, gb_ref, cw_ref, hm_ref,
                xq_ref, xk_ref, c_ref, n_ref, m_ref, lc=lc)
    _ret_step(rq_ref, rk_ref, rv_ref, rg_ref, hr_ref, r_ref, dm_ref, lc=lc)


def _scans(proj, gates, gate_bias, conv_w, l, B, S, lc):
    T = B * S
    nc = S // lc
    W = MLSTM_HEADS * MLSTM_HD
    blk = lambda c0: pl.BlockSpec((lc, W), lambda b, c: (b * nc + c, c0))
    return pl.pallas_call(
        functools.partial(_scan_kernel, lc=lc),
        grid=(B, nc),
        in_specs=[
            blk(0), blk(1), blk(2), blk(3),
            pl.BlockSpec((lc, GATE_PAD), lambda b, c: (b * nc + c, 0)),
            pl.BlockSpec((None, 1, GATE_PAD), lambda b, c: (l, 0, 0)),
            pl.BlockSpec((None, CONV_K, 2 * W), lambda b, c: (l, 0, 0)),
            blk(7), blk(8), blk(9), blk(10),
        ],
        out_specs=[blk(0), blk(0)],
        out_shape=[jax.ShapeDtypeStruct((T, W), BF16), jax.ShapeDtypeStruct((T, W), BF16)],
        scratch_shapes=[
            pltpu.VMEM((8, W), F32),
            pltpu.VMEM((8, W), F32),
            pltpu.VMEM((MLSTM_HEADS, MLSTM_HD, MLSTM_HD), F32),
            pltpu.VMEM((8, MLSTM_HD), F32),
            pltpu.VMEM((8, 128), F32),
            pltpu.VMEM((RET_HEADS, RET_HD, RET_HD), F32),
            pltpu.VMEM((RET_HEADS, lc, lc), F32),
        ],
        compiler_params=pltpu.CompilerParams(
            dimension_semantics=("arbitrary", "arbitrary"), vmem_limit_bytes=VMEM_LIMIT),
        name="scans",
    )(proj, proj, proj, proj, gates, gate_bias, conv_w, proj, proj, proj, proj)


def _dattn_kernel(q_ref, k_ref, v_ref, lam_ref, sub_ref, out_ref,
                  k1_ref, k2_ref, vp_ref, cd_ref, acc_ref, m_ref, qa_ref, s0_ref, s1_ref, mc0_ref, mc1_ref,
                  *, bq, bk, lam_init):
    s_refs = (s0_ref, s1_ref)
    mc_refs = (mc0_ref, mc1_ref)
    dh = DIFF_HD
    S = k_ref.shape[0]
    h = pl.program_id(1)
    qi = pl.program_id(2)
    slope = lax.bitcast_convert_type(jnp.full((1, 1), (126 - h) << 23, jnp.int32), F32)

    @pl.when(qi == 0)
    def _():
        lane = lax.broadcasted_iota(jnp.int32, (bk, 128), 1)
        rloc = lax.broadcasted_iota(jnp.int32, (bk, 128), 0)

        def build(i, carry):
            r0 = pl.multiple_of(i * bk, bk)
            t = rloc + r0
            fa = slope * (t - (t & (CHUNK - 1))).astype(F32)
            fb = slope * (t & (CHUNK - 1)).astype(F32)
            feat = jnp.where(lane < dh + 3, fa, jnp.where(lane < dh + 6, fb,
                                                          jnp.where(lane < dh + 9, 1.0, 0.0)))
            kf = k_ref[pl.ds(r0, bk), :].astype(F32)
            k1_ref[pl.ds(r0, bk), :] = jnp.where(lane < dh, kf, feat).astype(BF16)
            k2_ref[pl.ds(r0, bk), :] = jnp.where(lane < dh, pltpu.roll(kf, dh, 1), feat).astype(BF16)
            vp_ref[pl.ds(r0, bk), 0:128] = v_ref[pl.ds(r0, bk), :]
            vp_ref[pl.ds(r0, bk), 128:256] = jnp.ones((bk, 128), BF16)
            return carry

        lax.fori_loop(0, S // bk, build, 0)
        i2 = lax.broadcasted_iota(jnp.int32, (bq, bk), 0)
        for d in range(2):
            j2 = lax.broadcasted_iota(jnp.int32, (bq, bk), 1) + d * bk
            fut = jnp.where(j2 > i2, (-2.0 * LOG2E * slope) * (j2 - i2).astype(F32), 0.0)
            cd_ref[d] = jnp.where((j2 >> 6) <= (i2 >> 6), fut, NEG_BIG)

    lane = lax.broadcasted_iota(jnp.int32, (bq, 128), 1)
    c = (-LOG2E * (qi * bq).astype(F32)) * slope
    c1 = c.astype(BF16).astype(F32)
    c2 = (c - c1).astype(BF16).astype(F32)
    c3 = ((c - c1) - c2).astype(BF16).astype(F32)
    l1, l2, l3 = LOG2E_PIECES
    qfeat = jnp.where((lane == dh) | (lane == dh + 3), l1,
                      jnp.where((lane == dh + 1) | (lane == dh + 4), l2,
                                jnp.where((lane == dh + 2) | (lane == dh + 5), l3,
                                          jnp.where(lane == dh + 6, c1,
                                                    jnp.where(lane == dh + 7, c2,
                                                              jnp.where(lane == dh + 8, c3, 0.0))))))
    qf = q_ref[...].astype(F32) * (LOG2E * dh ** -0.5)
    qa_ref[0] = jnp.where(lane < dh, qf, qfeat).astype(BF16)
    qa_ref[1] = jnp.where(lane < dh, pltpu.roll(qf, dh, 1), qfeat).astype(BF16)
    kas = (k1_ref, k2_ref)

    m_ref[...] = jnp.full(m_ref.shape, NEG_BIG, F32)
    acc_ref[...] = jnp.zeros(acc_ref.shape, F32)

    full, upper = slice(0, bq), slice(bk, bq)

    def scores(blk, corr, slot, rows=full):
        k0 = pl.multiple_of(blk * bk, bk)
        nr = rows.stop - rows.start
        for c_ in range(2):
            sc = _dot_nt(qa_ref[c_, rows, :], kas[c_][pl.ds(k0, bk), :])
            if corr is not None:
                sc = sc + cd_ref[corr, rows, :]
            s_refs[slot][c_, rows, :] = sc
            mc_refs[slot][c_, rows, :] = jnp.broadcast_to(jnp.max(sc, axis=-1, keepdims=True), (nr, 128))

    def update(blk, slot, rows=full):
        vb = vp_ref[pl.ds(pl.multiple_of(blk * bk, bk), bk), :]
        for c_ in range(2):
            m_prev = m_ref[c_, rows, :]
            m_next = jnp.maximum(m_prev, mc_refs[slot][c_, rows, :])
            p = jnp.exp2(s_refs[slot][c_, rows, :] - jnp.concatenate([m_next] * (bk // 128), axis=1))
            alpha = jnp.exp2(m_prev - m_next)
            acc_ref[c_, rows, :] = (jnp.concatenate([alpha, alpha], axis=1) * acc_ref[c_, rows, :]
                                    + _dot(p.astype(BF16), vb))
            m_ref[c_, rows, :] = m_next

    scores(2 * qi, 0, 0)
    scores(2 * qi + 1, 1, 1, upper)
    update(2 * qi, 0)
    scores(0, None, 0)
    update(2 * qi + 1, 1, upper)

    def body(pr, carry):
        t = 2 * pr
        scores(t - 1, None, 1)
        update(t - 2, 0)
        scores(t, None, 0)
        update(t - 1, 1)
        return carry

    lax.fori_loop(1, qi, body, 0)

    @pl.when(qi >= 1)
    def _():
        t = 2 * qi
        scores(t - 1, None, 1)
        update(t - 2, 0)
        update(t - 1, 1)

    lm = lam_ref[...]
    lam = (jnp.exp(jnp.sum(lm[0:1, :] * lm[1:2, :], axis=-1, keepdims=True))
           - jnp.exp(jnp.sum(lm[2:3, :] * lm[3:4, :], axis=-1, keepdims=True)) + lam_init)
    a0 = acc_ref[0]
    a1 = acc_ref[1]
    o = a0[:, 0:128] / a0[:, 128:256] - lam * (a1[:, 0:128] / a1[:, 128:256])
    out_ref[...] = ((_rms(o) * sub_ref[...]) * (1.0 - lam_init)).astype(BF16)


def _dattn(proj, lam_p, subln, l, B, S, bq, lam_init):
    T = B * S
    nq = S // bq
    bk = bq // 2
    W = DIFF_HEADS * 2 * DIFF_HD
    return pl.pallas_call(
        functools.partial(_dattn_kernel, bq=bq, bk=bk, lam_init=lam_init),
        grid=(B, DIFF_HEADS, nq),
        in_specs=[
            pl.BlockSpec((bq, 128), lambda b, h, i: (b * nq + i, 32 + h)),
            pl.BlockSpec((S, 128), lambda b, h, i: (b, 40 + h)),
            pl.BlockSpec((S, 128), lambda b, h, i: (b, 48 + h)),
            pl.BlockSpec((None, 4, DIFF_HD), lambda b, h, i: (l, 0, 0)),
            pl.BlockSpec((None, 1, 128), lambda b, h, i: (l, 0, 0)),
        ],
        out_specs=pl.BlockSpec((bq, 128), lambda b, h, i: (b * nq + i, h)),
        out_shape=jax.ShapeDtypeStruct((T, W), BF16),
        scratch_shapes=[
            pltpu.VMEM((S, 128), BF16),
            pltpu.VMEM((S, 128), BF16),
            pltpu.VMEM((S, 256), BF16),
            pltpu.VMEM((2, bq, bk), F32),
            pltpu.VMEM((2, bq, 256), F32),
            pltpu.VMEM((2, bq, 128), F32),
            pltpu.VMEM((2, bq, 128), BF16),
            pltpu.VMEM((2, bq, bk), F32),
            pltpu.VMEM((2, bq, bk), F32),
            pltpu.VMEM((2, bq, 128), F32),
            pltpu.VMEM((2, bq, 128), F32),
        ],
        compiler_params=pltpu.CompilerParams(
            dimension_semantics=("arbitrary", "arbitrary", "arbitrary"),
            vmem_limit_bytes=VMEM_LIMIT),
        name="dattn",
    )(proj, proj, proj, lam_p, subln)


def _merge_kernel(hm_ref, hd_ref, hr_ref, gm_ref, gd_ref, gr_ref, x_ref,
                  wm_ref, wd_ref, wr_ref, wo_ref, g_ref, out_ref):
    y = _sigmoid(gm_ref[...].astype(F32)) * _dot(hm_ref[...], wm_ref[...])
    y = y + _sigmoid(gd_ref[...].astype(F32)) * _dot(hd_ref[...], wd_ref[...])
    y = y + _sigmoid(gr_ref[...].astype(F32)) * _dot(hr_ref[...], wr_ref[...])
    y2 = _dot(y.astype(BF16), wo_ref[...])
    out_ref[...] = x_ref[...] + _rms(y2) * g_ref[...]


def _merge(hm, hd, hr, proj, x2, wm, wd, wr, wo, gains, l, tm):
    T, D = x2.shape
    row = lambda c0: pl.BlockSpec((tm, D), lambda i: (i, c0))
    wspec = pl.BlockSpec((None, D, D), lambda i: (l, 0, 0))
    return pl.pallas_call(
        _merge_kernel,
        grid=(T // tm,),
        in_specs=[row(0), row(0), row(0), row(11), row(12), row(13), row(0),
                  wspec, wspec, wspec, wspec,
                  pl.BlockSpec((None, None, 1, D), lambda i: (l, 1, 0, 0))],
        out_specs=row(0),
        out_shape=jax.ShapeDtypeStruct((T, D), F32),
        compiler_params=pltpu.CompilerParams(
            dimension_semantics=("parallel",), vmem_limit_bytes=VMEM_LIMIT),
        name="merge",
    )(hm, hd, hr, proj, proj, proj, x2, wm, wd, wr, wo, gains)


def _ffn_kernel(x_ref, g2_ref, g3_ref, wa_ref, wu_ref, wo_ref, out_ref, *, hc):
    x = x_ref[...]
    h = (_rms(x) * g2_ref[...]).astype(BF16)
    f = None
    for c0 in range(0, FFN_HIDDEN, hc):
        sl = slice(c0, min(c0 + hc, FFN_HIDDEN))
        a = _dot(h, wa_ref[:, sl])
        u = _dot(h, wu_ref[:, sl])
        t = ((a * _sigmoid(a)) * u).astype(BF16)
        part = _dot(t, wo_ref[sl, :])
        f = part if f is None else f + part
    out_ref[...] = x + _rms(f) * g3_ref[...]


def _ffn(x2, gains, w_in, wo, l, tm, hc):
    T, D = x2.shape
    Hd = wo.shape[1]
    vec = lambda r: pl.BlockSpec((None, None, 1, D), lambda i: (l, r, 0, 0))
    return pl.pallas_call(
        functools.partial(_ffn_kernel, hc=hc),
        grid=(T // tm,),
        in_specs=[pl.BlockSpec((tm, D), lambda i: (i, 0)), vec(2), vec(3),
                  pl.BlockSpec((None, D, Hd), lambda i: (l, 0, 0)),
                  pl.BlockSpec((None, D, Hd), lambda i: (l, 0, 1)),
                  pl.BlockSpec((None, Hd, D), lambda i: (l, 0, 0))],
        out_specs=pl.BlockSpec((tm, D), lambda i: (i, 0)),
        out_shape=jax.ShapeDtypeStruct((T, D), F32),
        compiler_params=pltpu.CompilerParams(
            dimension_semantics=("parallel",), vmem_limit_bytes=VMEM_LIMIT),
        name="ffn",
    )(x2, gains, gains, w_in, w_in, wo)


def _tiles(B, S):
    T = B * S
    return dict(
        tm_in=min(2048, T), tn_in=1024,
        lc=min(256, S), bq=min(1024, S),
        tm_merge=min(512, T), tm_ffn=min(512, T), hc=6 * MXU_TILE,
    )


def kernel(x, norm_gains, w_in, mlstm_conv, mlstm_gate_bias, diff_lambda, diff_subln,
           w_mlstm_out, w_diff_out, w_ret_out, w_out, w_ffn_in, w_ffn_out):
    B, S, D = x.shape
    depth = w_in.shape[0]
    T = B * S
    cfg = _tiles(B, S)
    x2 = x.reshape(T, D).astype(F32)
    W = MLSTM_HEADS * MLSTM_HD

    gains = norm_gains.astype(F32).reshape(depth, 4, 1, D)
    w_main = jnp.concatenate([w_in[:, :, :4 * W], w_in[:, :, 4 * W + 8:]], axis=2).astype(BF16)
    w_if = jnp.pad(w_in[:, :, 4 * W:4 * W + 8], ((0, 0), (0, 0), (0, GATE_PAD - 8))).astype(BF16)
    gate_bias = jnp.pad(mlstm_gate_bias.astype(F32).reshape(depth, 1, 8), ((0, 0), (0, 0), (0, GATE_PAD - 8)))
    conv_w = mlstm_conv.astype(F32)
    lam_p = diff_lambda.astype(F32)
    subln = diff_subln.astype(F32).reshape(depth, 1, 2 * DIFF_HD)
    wm, wd, wr, wo = (w.astype(BF16) for w in (w_mlstm_out, w_diff_out, w_ret_out, w_out))
    wf_in, wf_out = w_ffn_in.astype(BF16), w_ffn_out.astype(BF16)

    for l in range(depth):
        proj, gates = _inproj(x2, gains, w_main, w_if, l, cfg["tm_in"], cfg["tn_in"])
        hm, hr = _scans(proj, gates, gate_bias, conv_w, l, B, S, cfg["lc"])
        lam_init = 0.8 - 0.6 * math.exp(-0.3 * l)
        hd = _dattn(proj, lam_p, subln, l, B, S, cfg["bq"], lam_init)
        x2 = _merge(hm, hd, hr, proj, x2, wm, wd, wr, wo, gains, l, cfg["tm_merge"])
        x2 = _ffn(x2, gains, wf_in, wf_out, l, cfg["tm_ffn"], cfg["hc"])

    return x2.reshape(B, S, D).astype(x.dtype)
```

```python
import functools
import math

import numpy as np
import jax
import jax.numpy as jnp
from jax import lax
from jax.experimental import pallas as pl
from jax.experimental.pallas import tpu as pltpu

F32 = jnp.float32
BF16 = jnp.bfloat16

EPS = 1e-6
CHUNK = 64
CHUNK_SHIFT = CHUNK.bit_length() - 1
LANE = 128
MLSTM_HEADS = 4
MLSTM_HD = 256
DIFF_HEADS = 8
DIFF_HD = 64
RET_HEADS = 4
RET_HD = 256
CONV_K = 4
FFN_HIDDEN = 2816
(COL_MQ, COL_MK, COL_MV, COL_MO, COL_DQ, COL_DK, COL_DV,
 COL_RQ, COL_RK, COL_RV, COL_RG, COL_GM, COL_GD, COL_GR) = range(14)
GATE_PAD = LANE
MXU_TILE = 256
NEG_BIG = -1e30
LOG2E = math.log2(math.e)


def _bf16_pieces(x, n=3):
    out, r = [], np.float64(x)
    for _ in range(n):
        piece = float(np.asarray(r, dtype=jnp.bfloat16).astype(np.float64))
        out.append(piece)
        r = r - piece
    return tuple(out)


LOG2E_PIECES = _bf16_pieces(LOG2E)

VMEM_LIMIT = 56 * 1024 * 1024


def _dot(a, b):
    return jnp.dot(a, b, preferred_element_type=F32)


def _dot_nt(a, b):
    return lax.dot_general(a, b, (((1,), (1,)), ((), ())), preferred_element_type=F32)


def _dot_tn(a, b):
    return lax.dot_general(a, b, (((0,), (0,)), ((), ())), preferred_element_type=F32)


def _sigmoid(x):
    return 1.0 / (1.0 + jnp.exp(-x))


def _rms(x, axis=-1):
    return x * lax.rsqrt(jnp.mean(x * x, axis=axis, keepdims=True) + EPS)


def _inproj_kernel(x_ref, g_ref, w_ref, wif_ref, o_ref, oif_ref, h_ref):
    @pl.when(pl.program_id(1) == 0)
    def _():
        x = x_ref[...]
        h = (_rms(x) * g_ref[...]).astype(BF16)
        h_ref[...] = h
        oif_ref[...] = _dot(h, wif_ref[...])

    o_ref[...] = _dot(h_ref[...], w_ref[...]).astype(BF16)


def _inproj(x2, gains, w_main, w_if, l, tm, tn):
    T, D = x2.shape
    n = w_main.shape[2]
    return pl.pallas_call(
        _inproj_kernel,
        grid=(T // tm, n // tn),
        in_specs=[
            pl.BlockSpec((tm, D), lambda i, j: (i, 0)),
            pl.BlockSpec((None, None, 1, D), lambda i, j: (l, 0, 0, 0)),
            pl.BlockSpec((None, D, tn), lambda i, j: (l, 0, j)),
            pl.BlockSpec((None, D, GATE_PAD), lambda i, j: (l, 0, 0)),
        ],
        out_specs=[
            pl.BlockSpec((tm, tn), lambda i, j: (i, j)),
            pl.BlockSpec((tm, GATE_PAD), lambda i, j: (i, 0)),
        ],
        out_shape=[
            jax.ShapeDtypeStruct((T, n), BF16),
            jax.ShapeDtypeStruct((T, GATE_PAD), F32),
        ],
        scratch_shapes=[pltpu.VMEM((tm, D), BF16)],
        compiler_params=pltpu.CompilerParams(
            dimension_semantics=("parallel", "arbitrary"), vmem_limit_bytes=VMEM_LIMIT),
        name="inproj",
    )(x2, gains, w_main, w_if)


def _split3(x):
    a = x.astype(BF16)
    r = x - a.astype(F32)
    b = r.astype(BF16)
    c = (r - b.astype(F32)).astype(BF16)
    return a, b, c


def _mlstm_step(q_ref, k_ref, v_ref, o_ref, gt_ref, gb_ref, cw_ref, out_ref,
                xq_ref, xk_ref, c_ref, n_ref, m_ref, *, lc):
    H, dh = MLSTM_HEADS, MLSTM_HD
    W = H * dh

    row = lax.broadcasted_iota(jnp.int32, (lc, lc), 0)
    col = lax.broadcasted_iota(jnp.int32, (lc, lc), 1)
    causal = row >= col
    rid = lax.broadcasted_iota(jnp.int32, (8, W), 0)
    shifts = [jnp.where(row - col == j, 1.0, 0.0).astype(BF16) for j in range(1, CONV_K)]

    def conv_silu(x_ref, tail_ref, w):
        xb = x_ref[...]
        z = xb.astype(F32) * w[3:4, :]
        for j in range(1, CONV_K):
            z = z + _dot(shifts[j - 1], xb) * w[3 - j:4 - j, :]
        tl = tail_ref[...]
        p1, p2, p3 = tl[7:8, :], tl[6:7, :], tl[5:6, :]
        head = jnp.where(rid == 0, p1 * w[2:3, :] + p2 * w[1:2, :] + p3 * w[0:1, :],
                         jnp.where(rid == 1, p1 * w[1:2, :] + p2 * w[0:1, :],
                                   jnp.where(rid == 2, p1 * w[0:1, :], 0.0)))
        z = z + jnp.concatenate([head, jnp.zeros((lc - 8, W), F32)], axis=0)
        tail_ref[...] = xb[lc - 8:lc, :].astype(F32)
        return z * _sigmoid(z)

    cw = cw_ref[...]
    qc = conv_silu(q_ref, xq_ref, cw[:, 0:W])
    kc = conv_silu(k_ref, xk_ref, cw[:, W:2 * W]) * (dh ** -0.5)

    g = gt_ref[...] + gb_ref[...]
    logf = jnp.minimum(g, 0.0) - jnp.log(1.0 + jnp.exp(-jnp.abs(g)))
    tril = jnp.where(causal, 1.0, 0.0).astype(BF16)
    f1, f2, f3 = _split3(logf)
    bc = _dot(tril, f1) + _dot(tril, f2) + _dot(tril, f3)
    bt = bc.T
    gtr = g.T

    for h in range(H):
        sl = slice(h * dh, (h + 1) * dh)
        qh = qc[:, sl]
        kh = kc[:, sl]
        qb = qh.astype(BF16)
        kb = kh.astype(BF16)
        vb = v_ref[:, sl]
        b_col = bc[:, 4 + h:5 + h]
        i_col = g[:, h:h + 1]
        b_row = bt[4 + h:5 + h, :]
        i_row = gtr[h:h + 1, :]
        gsum = bc[lc - 1:lc, 4 + h:5 + h]
        m_old = m_ref[h:h + 1, 0:1]
        c_old = c_ref[h]
        n_old = n_ref[h:h + 1, :]

        d = jnp.where(causal, b_col - b_row + i_row, NEG_BIG)
        d_inter = b_col + m_old
        m_row = jnp.maximum(jnp.max(d, axis=-1, keepdims=True), d_inter)
        w_intra = jnp.exp(d - m_row)
        w_inter = jnp.exp(d_inter - m_row)
        s = _dot_nt(qb, kb) * w_intra
        num = _dot(s.astype(BF16), vb) + _dot(qb, c_old.astype(BF16)) * w_inter
        den = (jnp.sum(s, axis=-1, keepdims=True)
               + jnp.sum(qh * n_old, axis=-1, keepdims=True) * w_inter)
        den = jnp.maximum(jnp.abs(den), jnp.exp(-m_row))
        hh = num / den
        og = o_ref[:, sl].astype(F32)
        out_ref[:, sl] = (_sigmoid(og) * _rms(hh)).astype(BF16)

        a_col = gsum - b_col + i_col
        m_new = jnp.maximum(gsum + m_old, jnp.max(a_col, axis=0, keepdims=True))
        w_old = jnp.exp(gsum + m_old - m_new)
        kw = kh * jnp.exp(a_col - m_new)
        c_ref[h] = w_old * c_old + _dot_tn(kw.astype(BF16), vb)
        n_ref[h:h + 1, :] = w_old * n_old + jnp.sum(kw, axis=0, keepdims=True)
        m_ref[h:h + 1, :] = jnp.broadcast_to(m_new, (1, LANE))


def _ret_decay_masks(dm_ref, lc):
    row = lax.broadcasted_iota(jnp.int32, (lc, lc), 0)
    col = lax.broadcasted_iota(jnp.int32, (lc, lc), 1)
    rel = (row - col).astype(F32)
    for h in range(RET_HEADS):
        lg = math.log(1.0 - 2.0 ** (-5.0 - h))
        dm_ref[h] = jnp.where(rel >= 0.0, jnp.exp(lg * jnp.maximum(rel, 0.0)), 0.0)


def _ret_step(q_ref, k_ref, v_ref, g_ref, out_ref, r_ref, dm_ref, *, lc):
    H, dh = RET_HEADS, RET_HD
    pos = lax.broadcasted_iota(jnp.int32, (lc, 1), 0).astype(F32)
    for h in range(H):
        lg = math.log(1.0 - 2.0 ** (-5.0 - h))
        sl = slice(h * dh, (h + 1) * dh)
        qb = q_ref[:, sl]
        kf = k_ref[:, sl].astype(F32) * (dh ** -0.5)
        vb = v_ref[:, sl]
        r_old = r_ref[h]
        xi = jnp.exp(lg * (pos + 1.0))
        zeta = jnp.exp(lg * (lc - 1.0 - pos))
        s = _dot_nt(qb, kf.astype(BF16)) * dm_ref[h]
        o = _dot(s.astype(BF16), vb) + _dot(qb, r_old.astype(BF16)) * xi
        r_ref[h] = math.exp(lg * lc) * r_old + _dot_tn((kf * zeta).astype(BF16), vb)
        gg = g_ref[:, sl].astype(F32)
        out_ref[:, sl] = (gg * _sigmoid(gg) * _rms(o)).astype(BF16)


def _scan_kernel(mq_ref, mk_ref, mv_ref, mo_ref, gt_ref, gb_ref, cw_ref, rq_ref, rk_ref, rv_ref, rg_ref,
                 hm_ref, hr_ref, xq_ref, xk_ref, c_ref, n_ref, m_ref, r_ref, dm_ref, *, lc):
    @pl.when(pl.program_id(1) == 0)
    def _():
        for ref in (xq_ref, xk_ref, c_ref, n_ref, m_ref, r_ref):
            ref[...] = jnp.zeros_like(ref)

    @pl.when((pl.program_id(0) == 0) & (pl.program_id(1) == 0))
    def _():
        _ret_decay_masks(dm_ref, lc)

    _mlstm_step(mq_ref, mk_ref, mv_ref, mo_ref, gt_ref, gb_ref, cw_ref, hm_ref,
                xq_ref, xk_ref, c_ref, n_ref, m_ref, lc=lc)
    _ret_step(rq_ref, rk_ref, rv_ref, rg_ref, hr_ref, r_ref, dm_ref, lc=lc)


def _scans(proj, gates, gate_bias, conv_w, l, B, S, lc):
    T = B * S
    nc = S // lc
    W = MLSTM_HEADS * MLSTM_HD
    blk = lambda c0: pl.BlockSpec((lc, W), lambda b, c: (b * nc + c, c0))
    return pl.pallas_call(
        functools.partial(_scan_kernel, lc=lc),
        grid=(B, nc),
        in_specs=[
            blk(COL_MQ), blk(COL_MK), blk(COL_MV), blk(COL_MO),
            pl.BlockSpec((lc, GATE_PAD), lambda b, c: (b * nc + c, 0)),
            pl.BlockSpec((None, 1, GATE_PAD), lambda b, c: (l, 0, 0)),
            pl.BlockSpec((None, CONV_K, 2 * W), lambda b, c: (l, 0, 0)),
            blk(COL_RQ), blk(COL_RK), blk(COL_RV), blk(COL_RG),
        ],
        out_specs=[blk(0), blk(0)],
        out_shape=[jax.ShapeDtypeStruct((T, W), BF16), jax.ShapeDtypeStruct((T, W), BF16)],
        scratch_shapes=[
            pltpu.VMEM((8, W), F32),
            pltpu.VMEM((8, W), F32),
            pltpu.VMEM((MLSTM_HEADS, MLSTM_HD, MLSTM_HD), F32),
            pltpu.VMEM((8, MLSTM_HD), F32),
            pltpu.VMEM((8, LANE), F32),
            pltpu.VMEM((RET_HEADS, RET_HD, RET_HD), F32),
            pltpu.VMEM((RET_HEADS, lc, lc), F32),
        ],
        compiler_params=pltpu.CompilerParams(
            dimension_semantics=("arbitrary", "arbitrary"), vmem_limit_bytes=VMEM_LIMIT),
        name="scans",
    )(proj, proj, proj, proj, gates, gate_bias, conv_w, proj, proj, proj, proj)


def _dattn_kernel(q_ref, k_ref, v_ref, lam_ref, sub_ref, out_ref,
                  k1_ref, k2_ref, vp_ref, cd_ref, acc_ref, m_ref, qa_ref, s0_ref, s1_ref, mc0_ref, mc1_ref,
                  *, bq, bk, lam_init):
    s_refs = (s0_ref, s1_ref)
    mc_refs = (mc0_ref, mc1_ref)
    dh = DIFF_HD
    S = k_ref.shape[0]
    h = pl.program_id(1)
    qi = pl.program_id(2)
    slope = lax.bitcast_convert_type(jnp.full((1, 1), (126 - h) << 23, jnp.int32), F32)

    @pl.when(qi == 0)
    def _():
        lane = lax.broadcasted_iota(jnp.int32, (bk, LANE), 1)
        rloc = lax.broadcasted_iota(jnp.int32, (bk, LANE), 0)

        def build(i, carry):
            r0 = pl.multiple_of(i * bk, bk)
            t = rloc + r0
            fa = slope * (t - (t & (CHUNK - 1))).astype(F32)
            fb = slope * (t & (CHUNK - 1)).astype(F32)
            feat = jnp.where(lane < dh + 3, fa, jnp.where(lane < dh + 6, fb,
                                                          jnp.where(lane < dh + 9, 1.0, 0.0)))
            kf = k_ref[pl.ds(r0, bk), :].astype(F32)
            k1_ref[pl.ds(r0, bk), :] = jnp.where(lane < dh, kf, feat).astype(BF16)
            k2_ref[pl.ds(r0, bk), :] = jnp.where(lane < dh, pltpu.roll(kf, dh, 1), feat).astype(BF16)
            vp_ref[pl.ds(r0, bk), 0:LANE] = v_ref[pl.ds(r0, bk), :]
            vp_ref[pl.ds(r0, bk), LANE:2 * LANE] = jnp.ones((bk, LANE), BF16)
            return carry

        lax.fori_loop(0, S // bk, build, 0)
        i2 = lax.broadcasted_iota(jnp.int32, (bq, bk), 0)
        for d in range(2):
            j2 = lax.broadcasted_iota(jnp.int32, (bq, bk), 1) + d * bk
            fut = jnp.where(j2 > i2, (-2.0 * LOG2E * slope) * (j2 - i2).astype(F32), 0.0)
            cd_ref[d] = jnp.where((j2 >> CHUNK_SHIFT) <= (i2 >> CHUNK_SHIFT), fut, NEG_BIG)

    lane = lax.broadcasted_iota(jnp.int32, (bq, LANE), 1)
    c = (-LOG2E * (qi * bq).astype(F32)) * slope
    c1 = c.astype(BF16).astype(F32)
    c2 = (c - c1).astype(BF16).astype(F32)
    c3 = ((c - c1) - c2).astype(BF16).astype(F32)
    l1, l2, l3 = LOG2E_PIECES
    qfeat = jnp.where((lane == dh) | (lane == dh + 3), l1,
                      jnp.where((lane == dh + 1) | (lane == dh + 4), l2,
                                jnp.where((lane == dh + 2) | (lane == dh + 5), l3,
                                          jnp.where(lane == dh + 6, c1,
                                                    jnp.where(lane == dh + 7, c2,
                                                              jnp.where(lane == dh + 8, c3, 0.0))))))
    qf = q_ref[...].astype(F32) * (LOG2E * dh ** -0.5)
    qa_ref[0] = jnp.where(lane < dh, qf, qfeat).astype(BF16)
    qa_ref[1] = jnp.where(lane < dh, pltpu.roll(qf, dh, 1), qfeat).astype(BF16)
    kas = (k1_ref, k2_ref)

    m_ref[...] = jnp.full(m_ref.shape, NEG_BIG, F32)
    acc_ref[...] = jnp.zeros(acc_ref.shape, F32)

    full, upper = slice(0, bq), slice(bk, bq)

    def scores(blk, corr, slot, rows=full):
        k0 = pl.multiple_of(blk * bk, bk)
        nr = rows.stop - rows.start
        for c_ in range(2):
            sc = _dot_nt(qa_ref[c_, rows, :], kas[c_][pl.ds(k0, bk), :])
            if corr is not None:
                sc = sc + cd_ref[corr, rows, :]
            s_refs[slot][c_, rows, :] = sc
            mc_refs[slot][c_, rows, :] = jnp.broadcast_to(jnp.max(sc, axis=-1, keepdims=True), (nr, LANE))

    def update(blk, slot, rows=full):
        vb = vp_ref[pl.ds(pl.multiple_of(blk * bk, bk), bk), :]
        for c_ in range(2):
            m_prev = m_ref[c_, rows, :]
            m_next = jnp.maximum(m_prev, mc_refs[slot][c_, rows, :])
            p = jnp.exp2(s_refs[slot][c_, rows, :] - jnp.concatenate([m_next] * (bk // LANE), axis=1))
            alpha = jnp.exp2(m_prev - m_next)
            acc_ref[c_, rows, :] = (jnp.concatenate([alpha, alpha], axis=1) * acc_ref[c_, rows, :]
                                    + _dot(p.astype(BF16), vb))
            m_ref[c_, rows, :] = m_next

    scores(2 * qi, 0, 0)
    scores(2 * qi + 1, 1, 1, upper)
    update(2 * qi, 0)
    scores(0, None, 0)
    update(2 * qi + 1, 1, upper)

    def body(pr, carry):
        t = 2 * pr
        scores(t - 1, None, 1)
        update(t - 2, 0)
        scores(t, None, 0)
        update(t - 1, 1)
        return carry

    lax.fori_loop(1, qi, body, 0)

    @pl.when(qi >= 1)
    def _():
        t = 2 * qi
        scores(t - 1, None, 1)
        update(t - 2, 0)
        update(t - 1, 1)

    lm = lam_ref[...]
    lam = (jnp.exp(jnp.sum(lm[0:1, :] * lm[1:2, :], axis=-1, keepdims=True))
           - jnp.exp(jnp.sum(lm[2:3, :] * lm[3:4, :], axis=-1, keepdims=True)) + lam_init)
    a0 = acc_ref[0]
    a1 = acc_ref[1]
    o = a0[:, :LANE] / a0[:, LANE:] - lam * (a1[:, :LANE] / a1[:, LANE:])
    out_ref[...] = ((_rms(o) * sub_ref[...]) * (1.0 - lam_init)).astype(BF16)


def _dattn(proj, lam_p, subln, l, B, S, bq, lam_init):
    T = B * S
    nq = S // bq
    bk = bq // 2
    W = DIFF_HEADS * 2 * DIFF_HD
    hpg = DIFF_HEADS
    return pl.pallas_call(
        functools.partial(_dattn_kernel, bq=bq, bk=bk, lam_init=lam_init),
        grid=(B, DIFF_HEADS, nq),
        in_specs=[
            pl.BlockSpec((bq, LANE), lambda b, h, i: (b * nq + i, hpg * COL_DQ + h)),
            pl.BlockSpec((S, LANE), lambda b, h, i: (b, hpg * COL_DK + h)),
            pl.BlockSpec((S, LANE), lambda b, h, i: (b, hpg * COL_DV + h)),
            pl.BlockSpec((None, 4, DIFF_HD), lambda b, h, i: (l, 0, 0)),
            pl.BlockSpec((None, 1, LANE), lambda b, h, i: (l, 0, 0)),
        ],
        out_specs=pl.BlockSpec((bq, LANE), lambda b, h, i: (b * nq + i, h)),
        out_shape=jax.ShapeDtypeStruct((T, W), BF16),
        scratch_shapes=[
            pltpu.VMEM((S, LANE), BF16),
            pltpu.VMEM((S, LANE), BF16),
            pltpu.VMEM((S, 2 * LANE), BF16),
            pltpu.VMEM((2, bq, bk), F32),
            pltpu.VMEM((2, bq, 2 * LANE), F32),
            pltpu.VMEM((2, bq, LANE), F32),
            pltpu.VMEM((2, bq, LANE), BF16),
            pltpu.VMEM((2, bq, bk), F32),
            pltpu.VMEM((2, bq, bk), F32),
            pltpu.VMEM((2, bq, LANE), F32),
            pltpu.VMEM((2, bq, LANE), F32),
        ],
        compiler_params=pltpu.CompilerParams(
            dimension_semantics=("arbitrary", "arbitrary", "arbitrary"),
            vmem_limit_bytes=VMEM_LIMIT),
        name="dattn",
    )(proj, proj, proj, lam_p, subln)


def _merge_kernel(hm_ref, hd_ref, hr_ref, gm_ref, gd_ref, gr_ref, x_ref,
                  wm_ref, wd_ref, wr_ref, wo_ref, g_ref, out_ref):
    y = _sigmoid(gm_ref[...].astype(F32)) * _dot(hm_ref[...], wm_ref[...])
    y = y + _sigmoid(gd_ref[...].astype(F32)) * _dot(hd_ref[...], wd_ref[...])
    y = y + _sigmoid(gr_ref[...].astype(F32)) * _dot(hr_ref[...], wr_ref[...])
    y2 = _dot(y.astype(BF16), wo_ref[...])
    out_ref[...] = x_ref[...] + _rms(y2) * g_ref[...]


def _merge(hm, hd, hr, proj, x2, wm, wd, wr, wo, gains, l, tm):
    T, D = x2.shape
    row = lambda c0: pl.BlockSpec((tm, D), lambda i: (i, c0))
    wspec = pl.BlockSpec((None, D, D), lambda i: (l, 0, 0))
    return pl.pallas_call(
        _merge_kernel,
        grid=(T // tm,),
        in_specs=[row(0), row(0), row(0), row(COL_GM), row(COL_GD), row(COL_GR), row(0),
                  wspec, wspec, wspec, wspec,
                  pl.BlockSpec((None, None, 1, D), lambda i: (l, 1, 0, 0))],
        out_specs=row(0),
        out_shape=jax.ShapeDtypeStruct((T, D), F32),
        compiler_params=pltpu.CompilerParams(
            dimension_semantics=("parallel",), vmem_limit_bytes=VMEM_LIMIT),
        name="merge",
    )(hm, hd, hr, proj, proj, proj, x2, wm, wd, wr, wo, gains)


def _ffn_kernel(x_ref, g2_ref, g3_ref, wa_ref, wu_ref, wo_ref, out_ref, *, hc):
    x = x_ref[...]
    h = (_rms(x) * g2_ref[...]).astype(BF16)
    f = None
    for c0 in range(0, FFN_HIDDEN, hc):
        sl = slice(c0, min(c0 + hc, FFN_HIDDEN))
        a = _dot(h, wa_ref[:, sl])
        u = _dot(h, wu_ref[:, sl])
        t = ((a * _sigmoid(a)) * u).astype(BF16)
        part = _dot(t, wo_ref[sl, :])
        f = part if f is None else f + part
    out_ref[...] = x + _rms(f) * g3_ref[...]


def _ffn(x2, gains, w_in, wo, l, tm, hc):
    T, D = x2.shape
    Hd = wo.shape[1]
    vec = lambda r: pl.BlockSpec((None, None, 1, D), lambda i: (l, r, 0, 0))
    return pl.pallas_call(
        functools.partial(_ffn_kernel, hc=hc),
        grid=(T // tm,),
        in_specs=[pl.BlockSpec((tm, D), lambda i: (i, 0)), vec(2), vec(3),
                  pl.BlockSpec((None, D, Hd), lambda i: (l, 0, 0)),
                  pl.BlockSpec((None, D, Hd), lambda i: (l, 0, 1)),
                  pl.BlockSpec((None, Hd, D), lambda i: (l, 0, 0))],
        out_specs=pl.BlockSpec((tm, D), lambda i: (i, 0)),
        out_shape=jax.ShapeDtypeStruct((T, D), F32),
        compiler_params=pltpu.CompilerParams(
            dimension_semantics=("parallel",), vmem_limit_bytes=VMEM_LIMIT),
        name="ffn",
    )(x2, gains, gains, w_in, w_in, wo)


def _tiles(B, S):
    T = B * S
    return dict(
        tm_in=min(2048, T), tn_in=1024,
        lc=min(256, S), bq=min(1024, S),
        tm_merge=min(512, T), tm_ffn=min(512, T), hc=6 * MXU_TILE,
    )


def kernel(x, norm_gains, w_in, mlstm_conv, mlstm_gate_bias, diff_lambda, diff_subln,
           w_mlstm_out, w_diff_out, w_ret_out, w_out, w_ffn_in, w_ffn_out):
    B, S, D = x.shape
    depth = w_in.shape[0]
    T = B * S
    cfg = _tiles(B, S)
    x2 = x.reshape(T, D).astype(F32)
    W = MLSTM_HEADS * MLSTM_HD

    gains = norm_gains.astype(F32).reshape(depth, 4, 1, D)
    w_main = jnp.concatenate([w_in[:, :, :4 * W], w_in[:, :, 4 * W + 8:]], axis=2).astype(BF16)
    w_if = jnp.pad(w_in[:, :, 4 * W:4 * W + 8], ((0, 0), (0, 0), (0, GATE_PAD - 8))).astype(BF16)
    gate_bias = jnp.pad(mlstm_gate_bias.astype(F32).reshape(depth, 1, 8), ((0, 0), (0, 0), (0, GATE_PAD - 8)))
    conv_w = mlstm_conv.astype(F32)
    lam_p = diff_lambda.astype(F32)
    subln = diff_subln.astype(F32).reshape(depth, 1, 2 * DIFF_HD)
    wm, wd, wr, wo = (w.astype(BF16) for w in (w_mlstm_out, w_diff_out, w_ret_out, w_out))
    wf_in, wf_out = w_ffn_in.astype(BF16), w_ffn_out.astype(BF16)

    for l in range(depth):
        proj, gates = _inproj(x2, gains, w_main, w_if, l, cfg["tm_in"], cfg["tn_in"])
        hm, hr = _scans(proj, gates, gate_bias, conv_w, l, B, S, cfg["lc"])
        lam_init = 0.8 - 0.6 * math.exp(-0.3 * l)
        hd = _dattn(proj, lam_p, subln, l, B, S, cfg["bq"], lam_init)
        x2 = _merge(hm, hd, hr, proj, x2, wm, wd, wr, wo, gains, l, cfg["tm_merge"])
        x2 = _ffn(x2, gains, wf_in, wf_out, l, cfg["tm_ffn"], cfg["hc"])

    return x2.reshape(B, S, D).astype(x.dtype)
```
